```python
import math
import jax, jax.numpy as jnp
from jax import lax
import numpy as np

D_MODEL = 2048
BATCH = 8
SEQ = 2048
DEPTH = 4
DEC_BATCH = 8
DEC_SEQ = 32
PAST_LEN = 1024

CHUNK = 64
H_R = 8
DK = D_MODEL // 16
DV = D_MODEL // 8
QKW = H_R * DK
RV = H_R * DV
D_MLP = D_MODEL
G_MLP = 8
DG = D_MLP // G_MLP
GMLP_CHUNK = 128
ROPE_BASE = 10000.0
EPS = 1e-6
IN_SIZES = (QKW, QKW, RV, RV, D_MLP, D_MLP, D_MLP, D_MODEL, D_MODEL)
N_IN = QKW * 2 + RV * 2 + D_MLP * 3 + D_MODEL * 2

kernel_name = "retention_gmlp_gated_stream_step"


def _split_points():
    pts, acc = [], 0
    for s in IN_SIZES[:-1]:
        acc += s
        pts.append(acc)
    return pts


def rmsnorm(x, g):
    xf = x.astype(jnp.float32)
    y = xf * lax.rsqrt(jnp.mean(xf * xf, axis=-1, keepdims=True) + EPS)
    return (y * g.astype(jnp.float32)).astype(x.dtype)


def layernorm(x, g, b):
    xf = x.astype(jnp.float32)
    mu = jnp.mean(xf, axis=-1, keepdims=True)
    var = jnp.mean(jnp.square(xf - mu), axis=-1, keepdims=True)
    y = (xf - mu) * lax.rsqrt(var + EPS)
    return (y * g.astype(jnp.float32) + b.astype(jnp.float32)).astype(x.dtype)


def head_groupnorm(o):
    of = o.astype(jnp.float32)
    mu = jnp.mean(of, axis=-1, keepdims=True)
    var = jnp.mean(jnp.square(of - mu), axis=-1, keepdims=True)
    return ((of - mu) * lax.rsqrt(var + EPS)).astype(o.dtype)


def rotary(x, pos):
    half = DK // 2
    inv_freq = 1.0 / (ROPE_BASE ** (jnp.arange(half, dtype=jnp.float32) / half))
    ang = pos.astype(jnp.float32)[:, None] * inv_freq[None, :]
    cos = jnp.cos(ang)[None, :, None, :].astype(x.dtype)
    sin = jnp.sin(ang)[None, :, None, :].astype(x.dtype)
    x1, x2 = x[..., :half], x[..., half:]
    return jnp.concatenate([x1 * cos - x2 * sin, x1 * sin + x2 * cos], axis=-1)


def retention(q, k, v, s0, L):
    B, S, H, _ = q.shape
    nc = S // L
    dt = q.dtype
    log_gamma = jnp.log(1.0 - 2.0 ** (-5.0 - jnp.arange(H, dtype=jnp.float32)))
    idx = jnp.arange(L, dtype=jnp.float32)
    dist = jnp.abs(idx[:, None] - idx[None, :])
    intra_decay = jnp.exp(log_gamma[:, None, None] * dist[None]).astype(dt)
    q_dec = jnp.exp(log_gamma[None, :] * (idx[:, None] + 1.0))[..., None].astype(dt)
    k_dec = jnp.exp(log_gamma[None, :] * (L - 1.0 - idx[:, None]))[..., None].astype(dt)
    chunk_dec = jnp.exp(log_gamma * L)[None, :, None, None]

    def to_chunks(t):
        return jnp.moveaxis(t.reshape(B, nc, L, H, t.shape[-1]), 1, 0)

    qc, kc, vc = to_chunks(q), to_chunks(k), to_chunks(v)

    def step(state, inp):
        qi, ki, vi = inp
        scores = jnp.einsum('bihd,bjhd->bhij', qi, ki) * intra_decay[None]
        intra = jnp.einsum('bhij,bjhe->bihe', scores, vi)
        cross = jnp.einsum('bihd,bhde->bihe', qi * q_dec, state)
        new_state = (chunk_dec.astype(state.dtype) * state
                     + jnp.einsum('bjhd,bjhe->bhde', ki * k_dec, vi)).astype(state.dtype)
        return new_state, intra + cross

    s_fin, out = lax.scan(step, s0, (qc, kc, vc))
    out = jnp.moveaxis(out, 0, 1).reshape(B, S, H, v.shape[-1])
    return out, s_fin


def spatial_gate(u, vn, ws, ws_b):
    B, S, _ = u.shape
    L = GMLP_CHUNK if S >= GMLP_CHUNK else S
    ncg = S // L
    i = jnp.arange(L)
    mask = (i[:, None] // CHUNK) >= (i[None, :] // CHUNK)
    w = jnp.where(mask[None], ws[:, :L, :L], 0.0).astype(vn.dtype)
    vr = vn.reshape(B, ncg, L, G_MLP, DG)
    bias = jnp.transpose(ws_b[:, :L])[None, None, :, :, None].astype(vn.dtype)
    s = jnp.einsum('gij,bcjgd->bcigd', w, vr) + bias
    return u * s.reshape(B, S, D_MLP)


def mixer_layer(x, s0, pos, ret_chunk, norm_g, w_in, ws, ws_b, ln_g, ln_b,
                w_ret_out, w_mlp_out, w_o):
    B, S, _ = x.shape
    h = rmsnorm(x, norm_g)
    z = h @ w_in
    q, k, v, gr, u, vm, gm, ar, am = jnp.split(z, _split_points(), axis=-1)
    q = rotary(q.reshape(B, S, H_R, DK), pos)
    k = rotary(k.reshape(B, S, H_R, DK), pos) * (DK ** -0.5)
    v = v.reshape(B, S, H_R, DV)
    o, s_new = retention(q, k, v, s0, ret_chunk)
    o = head_groupnorm(o).reshape(B, S, RV)
    r_branch = (jax.nn.silu(gr) * o) @ w_ret_out
    u = jax.nn.gelu(u, approximate=False)
    vn = layernorm(jax.nn.gelu(vm, approximate=False), ln_g, ln_b)
    m = spatial_gate(u, vn, ws, ws_b)
    m_branch = (jax.nn.silu(gm) * m) @ w_mlp_out
    merged = jax.nn.sigmoid(ar) * r_branch + jax.nn.sigmoid(am) * m_branch
    return x + merged @ w_o, s_new, vn


def setup_inputs(seed: int = 0) -> dict:
    key = jax.random.key(seed)
    ks = jax.random.split(key, 14)
    f32 = jnp.float32
    nrm = lambda k, shape, scale: jax.random.normal(k, shape, f32) * scale
    return {
        "x_prompt": nrm(ks[0], (BATCH, SEQ, D_MODEL), 1.0),
        "x_sample": nrm(ks[1], (DEC_BATCH, DEC_SEQ, D_MODEL), 1.0),
        "state_ret": nrm(ks[2], (DEPTH, DEC_BATCH, H_R, DK, DV), 0.5),
        "norm_g": 1.0 + nrm(ks[3], (DEPTH, D_MODEL), 0.02),
        "w_in": nrm(ks[4], (DEPTH, D_MODEL, N_IN), D_MODEL ** -0.5),
        "ws": nrm(ks[5], (DEPTH, G_MLP, GMLP_CHUNK, GMLP_CHUNK), GMLP_CHUNK ** -0.5),
        "ws_b": 1.0 + nrm(ks[6], (DEPTH, G_MLP, GMLP_CHUNK), 0.1),
        "ln_g": 1.0 + nrm(ks[7], (DEPTH, D_MLP), 0.02),
        "ln_b": nrm(ks[8], (DEPTH, D_MLP), 0.01),
        "w_ret_out": nrm(ks[9], (DEPTH, RV, D_MODEL), RV ** -0.5),
        "w_mlp_out": nrm(ks[10], (DEPTH, D_MLP, D_MODEL), D_MLP ** -0.5),
        "w_o": nrm(ks[11], (DEPTH, D_MODEL, D_MODEL), D_MODEL ** -0.5),
        "final_g": 1.0 + nrm(ks[12], (D_MODEL,), 0.02),
    }


def reference(x_prompt, x_sample, state_ret, norm_g, w_in, ws, ws_b, ln_g, ln_b,
              w_ret_out, w_mlp_out, w_o, final_g):
    b_p, s_p = x_prompt.shape[0], x_prompt.shape[1]
    s_s = x_sample.shape[1]
    pos_p = jnp.arange(s_p)
    pos_s = PAST_LEN + jnp.arange(s_s)
    hp, hs = x_prompt, x_sample
    st_p, st_s, v_s = [], [], []
    for l in range(DEPTH):
        w_l = (norm_g[l], w_in[l], ws[l], ws_b[l], ln_g[l], ln_b[l],
               w_ret_out[l], w_mlp_out[l], w_o[l])
        s0 = jnp.zeros((b_p, H_R, DK, DV), hp.dtype)
        hp, sp, _ = mixer_layer(hp, s0, pos_p, CHUNK, *w_l)
        hs, ss, vs = mixer_layer(hs, state_ret[l], pos_s, s_s, *w_l)
        st_p.append(sp)
        st_s.append(ss)
        v_s.append(vs)
    y_prompt = rmsnorm(hp, final_g)
    y_sample = rmsnorm(hs, final_g)
    new_state_ret_prompt = jnp.stack(st_p)
    new_state_ret_sample = jnp.stack(st_s)
    new_gmlp_v_sample = jnp.stack(v_s)
    return (y_prompt, y_sample, new_state_ret_prompt, new_state_ret_sample, new_gmlp_v_sample)
```

```python
import functools

import jax
import jax.numpy as jnp
import numpy as np
from jax import lax
from jax.experimental import pallas as pl
from jax.experimental.pallas import tpu as pltpu

D_MODEL = 2048
DEPTH = 4
PAST_LEN = 1024
CHUNK = 64
H_R = 8
DK = D_MODEL // 16
DV = D_MODEL // 8
QKW = H_R * DK
RV = H_R * DV
D_MLP = D_MODEL
G_MLP = 8
DG = D_MLP // G_MLP
GMLP_CHUNK = 128
ROPE_BASE = 10000.0
EPS = 1e-6
N_IN = QKW * 2 + RV * 2 + D_MLP * 3 + D_MODEL * 2

OFF_Q = 0
OFF_K = OFF_Q + QKW
OFF_V = OFF_K + QKW
OFF_GR = OFF_V + RV
OFF_U = OFF_GR + RV
OFF_VM = OFF_U + D_MLP
OFF_GM = OFF_VM + D_MLP
OFF_AR = OFF_GM + D_MLP
OFF_AM = OFF_AR + D_MODEL

V7X_VMEM_BYTES = 64 * 1024 * 1024
VMEM_LIMIT = 56 * 1024 * 1024

F32 = jnp.float32
BF16 = jnp.bfloat16


def _rmsnorm(x, g):
    return x * lax.rsqrt(jnp.mean(x * x, axis=-1, keepdims=True) + EPS) * g


def _gelu(x):
    return 0.5 * x * (1.0 + lax.erf(x * np.float32(np.sqrt(0.5))))


def _silu(x):
    return x * jax.nn.sigmoid(x)


def _in_proj_kernel(x_ref, g_ref, w_ref, z_ref, h_scr):
    @pl.when(pl.program_id(1) == 0)
    def _():
        h_scr[...] = _rmsnorm(x_ref[...], g_ref[...]).astype(BF16)

    z_ref[...] = jnp.dot(h_scr[...], w_ref[...], preferred_element_type=F32)


def _in_proj(x, g, w, tm, tn):
    t = x.shape[0]
    return pl.pallas_call(
        _in_proj_kernel,
        grid=(t // tm, N_IN // tn),
        in_specs=[
            pl.BlockSpec((tm, D_MODEL), lambda i, j: (i, 0)),
            pl.BlockSpec((1, D_MODEL), lambda i, j: (0, 0)),
            pl.BlockSpec((D_MODEL, tn), lambda i, j: (0, j)),
        ],
        out_specs=pl.BlockSpec((tm, tn), lambda i, j: (i, j)),
        out_shape=jax.ShapeDtypeStruct((t, N_IN), F32),
        scratch_shapes=[pltpu.VMEM((tm, D_MODEL), BF16)],
        compiler_params=pltpu.CompilerParams(
            dimension_semantics=("parallel", "arbitrary"),
            vmem_limit_bytes=VMEM_LIMIT),
        name="in_proj",
    )(x, g, w)


def _retention_kernel(q_ref, k_ref, v_ref, gr_ref, cos_ref, sin_ref, mask_ref,
                      qdec_ref, kdec_ref, sdec_ref, s0_ref, o_ref, s_ref):
    @pl.when(pl.program_id(2) == 0)
    def _():
        s_ref[...] = s0_ref[...]

    cos = cos_ref[...]
    sin = sin_ref[...]
    q = q_ref[...]
    k = k_ref[...]
    qr = q * cos + pltpu.roll(q, DK // 2, 1) * sin
    kr = (k * cos + pltpu.roll(k, DK // 2, 1) * sin) * np.float32(DK ** -0.5)
    qb = qr.astype(BF16)
    kb = kr.astype(BF16)
    vb = v_ref[...].astype(BF16)
    state = s_ref[0, 0]

    scores = lax.dot_general(qb, kb, (((1,), (1,)), ((), ())),
                             preferred_element_type=F32)
    p = (scores * mask_ref[0]).astype(BF16)
    intra = jnp.dot(p, vb, preferred_element_type=F32)
    cross = jnp.dot((qr * qdec_ref[0]).astype(BF16), state.astype(BF16),
                    preferred_element_type=F32)
    o = intra + cross

    kd = (kr * kdec_ref[0]).astype(BF16)
    s_ref[0, 0] = sdec_ref[0] * state + lax.dot_general(
        kd, vb, (((0,), (0,)), ((), ())), preferred_element_type=F32)

    mu = jnp.mean(o, axis=-1, keepdims=True)
    d = o - mu
    var = jnp.mean(d * d, axis=-1, keepdims=True)
    on = d * lax.rsqrt(var + EPS)
    o_ref[...] = (_silu(gr_ref[...]) * on).astype(BF16)


def _retention_tables(seq, ts, chunk, pos0):
    log_gamma = jnp.log(1.0 - 2.0 ** (-5.0 - jnp.arange(H_R, dtype=F32)))
    idx = jnp.arange(ts, dtype=F32)
    diff = idx[:, None] - idx[None, :]
    cn = (jnp.arange(ts) // chunk)[:, None]
    cm = (jnp.arange(ts) // chunk)[None, :]
    expo = jnp.where(cn == cm, jnp.abs(diff), diff)
    mask = jnp.where((cn >= cm)[None],
                     jnp.exp(log_gamma[:, None, None] * expo[None]), 0.0)
    ones = jnp.ones((1, 1, DK), F32)
    qdec = jnp.exp(log_gamma[:, None] * (idx[None, :] + 1.0))[..., None] * ones
    kdec = jnp.exp(log_gamma[:, None] * (ts - 1.0 - idx[None, :]))[..., None] * ones
    sdec = jnp.exp(log_gamma * ts)[:, None, None] * jnp.ones((1, 1, DV), F32)
    half = DK // 2
    inv_freq = 1.0 / (ROPE_BASE ** (jnp.arange(half, dtype=F32) / half))
    pos = (pos0 + jnp.arange(seq)).astype(F32)
    ang = pos[:, None] * inv_freq[None, :]
    cos = jnp.cos(ang)
    sin = jnp.sin(ang)
    cos2 = jnp.concatenate([cos, cos], axis=-1)
    sin2 = jnp.concatenate([-sin, sin], axis=-1)
    return cos2, sin2, mask, qdec, kdec, sdec


def _retention(z, s0, tables, batch, seq, ts):
    cos2, sin2, mask, qdec, kdec, sdec = tables
    ns = seq // ts
    t = batch * seq
    row = lambda b, h, s: b * ns + s
    return pl.pallas_call(
        _retention_kernel,
        grid=(batch, H_R, ns),
        in_specs=[
            pl.BlockSpec((ts, DK), lambda b, h, s: (row(b, h, s), OFF_Q // DK + h)),
            pl.BlockSpec((ts, DK), lambda b, h, s: (row(b, h, s), OFF_K // DK + h)),
            pl.BlockSpec((ts, DV), lambda b, h, s: (row(b, h, s), OFF_V // DV + h)),
            pl.BlockSpec((ts, DV), lambda b, h, s: (row(b, h, s), OFF_GR // DV + h)),
            pl.BlockSpec((ts, DK), lambda b, h, s: (s, 0)),
            pl.BlockSpec((ts, DK), lambda b, h, s: (s, 0)),
            pl.BlockSpec((1, ts, ts), lambda b, h, s: (h, 0, 0)),
            pl.BlockSpec((1, ts, DK), lambda b, h, s: (h, 0, 0)),
            pl.BlockSpec((1, ts, DK), lambda b, h, s: (h, 0, 0)),
            pl.BlockSpec((1, 1, DV), lambda b, h, s: (h, 0, 0)),
            pl.BlockSpec((1, 1, DK, DV), lambda b, h, s: (b, h, 0, 0)),
        ],
        out_specs=[
            pl.BlockSpec((ts, DV), lambda b, h, s: (row(b, h, s), h)),
            pl.BlockSpec((1, 1, DK, DV), lambda b, h, s: (b, h, 0, 0)),
        ],
        out_shape=[
            jax.ShapeDtypeStruct((t, RV), BF16),
            jax.ShapeDtypeStruct((batch, H_R, DK, DV), F32),
        ],
        compiler_params=pltpu.CompilerParams(
            dimension_semantics=("parallel", "parallel", "arbitrary"),
            vmem_limit_bytes=VMEM_LIMIT),
        name="retention",
    )(z, z, z, z, cos2, sin2, mask, qdec, kdec, sdec, s0)


def _gmlp_kernel(u_ref, vm_ref, gm_ref, ws_ref, bias_ref, lng_ref, lnb_ref,
                 m_ref, vn_ref, w2_scr, *, ts, lg):
    @pl.when((pl.program_id(0) == 0) & (pl.program_id(1) == 0))
    def _():
        i = lax.broadcasted_iota(jnp.int32, (lg, lg), 0)
        j = lax.broadcasted_iota(jnp.int32, (lg, lg), 1)
        keep = (i // CHUNK) >= (j // CHUNK)
        w2_scr[...] = jnp.zeros_like(w2_scr)
        for g in range(G_MLP):
            wg = jnp.where(keep, ws_ref[g, :lg, :lg], 0.0).astype(BF16)
            for c in range(ts // lg):
                w2_scr[g, c * lg:(c + 1) * lg, c * lg:(c + 1) * lg] = wg

    gv = _gelu(vm_ref[...])
    mu = jnp.mean(gv, axis=-1, keepdims=True)
    d = gv - mu
    var = jnp.mean(d * d, axis=-1, keepdims=True)
    vn = d * lax.rsqrt(var + EPS) * lng_ref[...] + lnb_ref[...]
    vn_ref[...] = vn
    for g in range(G_MLP):
        sl = slice(g * DG, (g + 1) * DG)
        s = jnp.dot(w2_scr[g], vn[:, sl].astype(BF16),
                    preferred_element_type=F32) + bias_ref[:, g:g + 1]
        m = _gelu(u_ref[:, sl]) * s
        m_ref[:, sl] = (_silu(gm_ref[:, sl]) * m).astype(BF16)


def _gmlp(z, ws, ws_b, ln_g, ln_b, batch, seq, ts):
    lg = min(GMLP_CHUNK, seq)
    ns = seq // ts
    t = batch * seq
    bias = jnp.tile(jnp.transpose(ws_b[:, :lg]), (ts // lg, 1))
    row = lambda b, s: b * ns + s
    return pl.pallas_call(
        functools.partial(_gmlp_kernel, ts=ts, lg=lg),
        grid=(batch, ns),
        in_specs=[
            pl.BlockSpec((ts, D_MLP), lambda b, s: (row(b, s), OFF_U // D_MLP)),
            pl.BlockSpec((ts, D_MLP), lambda b, s: (row(b, s), OFF_VM // D_MLP)),
            pl.BlockSpec((ts, D_MLP), lambda b, s: (row(b, s), OFF_GM // D_MLP)),
            pl.BlockSpec((G_MLP, GMLP_CHUNK, GMLP_CHUNK), lambda b, s: (0, 0, 0)),
            pl.BlockSpec((ts, G_MLP), lambda b, s: (0, 0)),
            pl.BlockSpec((1, D_MLP), lambda b, s: (0, 0)),
            pl.BlockSpec((1, D_MLP), lambda b, s: (0, 0)),
        ],
        out_specs=[
            pl.BlockSpec((ts, D_MLP), lambda b, s: (row(b, s), 0)),
            pl.BlockSpec((ts, D_MLP), lambda b, s: (row(b, s), 0)),
        ],
        out_shape=[
            jax.ShapeDtypeStruct((t, D_MLP), BF16),
            jax.ShapeDtypeStruct((t, D_MLP), F32),
        ],
        scratch_shapes=[pltpu.VMEM((G_MLP, ts, ts), BF16)],
        compiler_params=pltpu.CompilerParams(
            dimension_semantics=("arbitrary", "arbitrary"),
            vmem_limit_bytes=VMEM_LIMIT),
        name="gmlp",
    )(z, z, z, ws, bias, ln_g, ln_b)


def _merge_out_kernel(r_ref, m_ref, ar_ref, am_ref, x_ref, wr_ref, wm_ref,
                      wo_ref, fg_ref, y_ref, *, final_norm):
    rb = jnp.dot(r_ref[...], wr_ref[...], preferred_element_type=F32)
    mb = jnp.dot(m_ref[...], wm_ref[...], preferred_element_type=F32)
    merged = jax.nn.sigmoid(ar_ref[...]) * rb + jax.nn.sigmoid(am_ref[...]) * mb
    y = x_ref[...] + jnp.dot(merged.astype(BF16), wo_ref[...],
                             preferred_element_type=F32)
    if final_norm:
        y = _rmsnorm(y, fg_ref[...])
    y_ref[...] = y


def _merge_out(r_act, m_act, z, x, w_ret, w_mlp, w_o, final_g, tm, final_norm):
    t = x.shape[0]
    resident = lambda: pl.BlockSpec((D_MODEL, D_MODEL), lambda i: (0, 0),
                                    pipeline_mode=pl.Buffered(1))
    return pl.pallas_call(
        functools.partial(_merge_out_kernel, final_norm=final_norm),
        grid=(t // tm,),
        in_specs=[
            pl.BlockSpec((tm, RV), lambda i: (i, 0)),
            pl.BlockSpec((tm, D_MLP), lambda i: (i, 0)),
            pl.BlockSpec((tm, D_MODEL), lambda i: (i, OFF_AR // D_MODEL)),
            pl.BlockSpec((tm, D_MODEL), lambda i: (i, OFF_AM // D_MODEL)),
            pl.BlockSpec((tm, D_MODEL), lambda i: (i, 0)),
            resident(), resident(), resident(),
            pl.BlockSpec((1, D_MODEL), lambda i: (0, 0)),
        ],
        out_specs=pl.BlockSpec((tm, D_MODEL), lambda i: (i, 0)),
        out_shape=jax.ShapeDtypeStruct((t, D_MODEL), F32),
        compiler_params=pltpu.CompilerParams(
            dimension_semantics=("parallel",),
            vmem_limit_bytes=VMEM_LIMIT),
        name="merge_out",
    )(r_act, m_act, z, z, x, w_ret, w_mlp, w_o, final_g)


def _mixer_layer(x, s0, tables, batch, seq, ts, tm, tn, tmo, wl, final_g, final_norm):
    norm_g, w_in, ws, ws_b, ln_g, ln_b, w_ret, w_mlp, w_o = wl
    z = _in_proj(x, norm_g, w_in, tm, tn)
    r_act, s_new = _retention(z, s0, tables, batch, seq, ts)
    m_act, vn = _gmlp(z, ws, ws_b, ln_g, ln_b, batch, seq, ts)
    y = _merge_out(r_act, m_act, z, x, w_ret, w_mlp, w_o, final_g, tmo, final_norm)
    return y, s_new, vn


def kernel(x_prompt, x_sample, state_ret, norm_g, w_in, ws, ws_b, ln_g, ln_b,
           w_ret_out, w_mlp_out, w_o, final_g):
    b_p, s_p, _ = x_prompt.shape
    b_s, s_s, _ = x_sample.shape
    ts_p = 256
    tab_p = _retention_tables(s_p, ts_p, CHUNK, 0)
    tab_s = _retention_tables(s_s, s_s, s_s, PAST_LEN)
    hp = x_prompt.reshape(b_p * s_p, D_MODEL)
    hs = x_sample.reshape(b_s * s_s, D_MODEL)
    zeros_state = jnp.zeros((b_p, H_R, DK, DV), F32)
    fg = final_g.reshape(1, D_MODEL)
    st_p, st_s, v_s = [], [], []
    for l in range(DEPTH):
        wl = (norm_g[l].reshape(1, D_MODEL), w_in[l].astype(BF16), ws[l], ws_b[l],
              ln_g[l].reshape(1, D_MLP), ln_b[l].reshape(1, D_MLP),
              w_ret_out[l].astype(BF16), w_mlp_out[l].astype(BF16),
              w_o[l].astype(BF16))
        last = l == DEPTH - 1
        hp, sp, _ = _mixer_layer(hp, zeros_state, tab_p, b_p, s_p, ts_p,
                                 1024, 1024, 256, wl, fg, last)
        hs, ss, vs = _mixer_layer(hs, state_ret[l], tab_s, b_s, s_s, s_s,
                                  256, 1024, 256, wl, fg, last)
        st_p.append(sp)
        st_s.append(ss)
        v_s.append(vs.reshape(b_s, s_s, D_MLP))
    return (hp.reshape(b_p, s_p, D_MODEL), hs.reshape(b_s, s_s, D_MODEL),
            jnp.stack(st_p), jnp.stack(st_s), jnp.stack(v_s))
```

```python
import functools

import jax
import jax.numpy as jnp
import numpy as np
from jax import lax
from jax.experimental import pallas as pl
from jax.experimental.pallas import tpu as pltpu

D_MODEL = 2048
DEPTH = 4
PAST_LEN = 1024
CHUNK = 64
H_R = 8
DK = D_MODEL // 16
DV = D_MODEL // 8
QKW = H_R * DK
RV = H_R * DV
D_MLP = D_MODEL
G_MLP = 8
DG = D_MLP // G_MLP
GMLP_CHUNK = 128
ROPE_BASE = 10000.0
EPS = 1e-6
N_IN = QKW * 2 + RV * 2 + D_MLP * 3 + D_MODEL * 2

OFF_Q = 0
OFF_K = OFF_Q + QKW
OFF_V = OFF_K + QKW
OFF_GR = OFF_V + RV
OFF_U = OFF_GR + RV
OFF_VM = OFF_U + D_MLP
OFF_GM = OFF_VM + D_MLP
OFF_AR = OFF_GM + D_MLP
OFF_AM = OFF_AR + D_MODEL

VMEM_LIMIT = 56 * 1024 * 1024

TM_IN = 1280
TN_IN = 1024
TS_RET = 256
TS_MLP = 256
TM_OUT = 256

F32 = jnp.float32
BF16 = jnp.bfloat16


def _rmsnorm(x, g):
    return x * lax.rsqrt(jnp.mean(x * x, axis=-1, keepdims=True) + EPS) * g


def _gelu(x):
    return 0.5 * x * (1.0 + lax.erf(x * np.float32(np.sqrt(0.5))))


def _silu(x):
    return x * jax.nn.sigmoid(x)


def _params(*sem):
    return pltpu.CompilerParams(dimension_semantics=sem,
                                vmem_limit_bytes=VMEM_LIMIT)


def _prenorm_kernel(x_ref, g_ref, h_ref):
    h_ref[...] = _rmsnorm(x_ref[...], g_ref[0]).astype(BF16)


def _prenorm(x, norm_g, layer):
    t = x.shape[0]
    return pl.pallas_call(
        _prenorm_kernel,
        grid=(t // TM_IN,),
        in_specs=[
            pl.BlockSpec((TM_IN, D_MODEL), lambda i: (i, 0)),
            pl.BlockSpec((1, 1, D_MODEL), lambda i: (layer, 0, 0)),
        ],
        out_specs=pl.BlockSpec((TM_IN, D_MODEL), lambda i: (i, 0)),
        out_shape=jax.ShapeDtypeStruct((t, D_MODEL), BF16),
        compiler_params=_params("parallel"),
        name="prenorm",
    )(x, norm_g)


def _in_proj_kernel(h_ref, w_ref, z_ref, wb_scr):
    @pl.when(pl.program_id(1) == 0)
    def _():
        wb_scr[...] = w_ref[...].astype(BF16)

    z_ref[...] = jnp.dot(h_ref[...], wb_scr[...], preferred_element_type=F32)


def _in_proj(h, w_in, layer):
    t = h.shape[0]
    return pl.pallas_call(
        _in_proj_kernel,
        grid=(N_IN // TN_IN, t // TM_IN),
        in_specs=[
            pl.BlockSpec((TM_IN, D_MODEL), lambda j, i: (i, 0)),
            pl.BlockSpec((None, D_MODEL, TN_IN), lambda j, i: (layer, 0, j)),
        ],
        out_specs=pl.BlockSpec((TM_IN, TN_IN), lambda j, i: (i, j)),
        out_shape=jax.ShapeDtypeStruct((t, N_IN), F32),
        scratch_shapes=[pltpu.VMEM((D_MODEL, TN_IN), BF16)],
        compiler_params=_params("arbitrary", "arbitrary"),
        name="in_proj",
    )(h, w_in)


def _retention_kernel(q_ref, k_ref, v_ref, gr_ref, cos_ref, sin_ref, mask_ref,
                      qdec_ref, kdec_ref, sdec_ref, s0_ref, o_ref, s_ref):
    @pl.when(pl.program_id(1) == 0)
    def _():
        s_ref[...] = s0_ref[...]

    cos = cos_ref[...]
    sin = sin_ref[...]
    for h in range(H_R):
        ks = slice(h * DK, (h + 1) * DK)
        vs = slice(h * DV, (h + 1) * DV)
        q = q_ref[:, ks]
        k = k_ref[:, ks]
        qr = q * cos + pltpu.roll(q, DK // 2, 1) * sin
        kr = (k * cos + pltpu.roll(k, DK // 2, 1) * sin) * np.float32(DK ** -0.5)
        qb = qr.astype(BF16)
        kb = kr.astype(BF16)
        vb = v_ref[:, vs].astype(BF16)
        state = s_ref[0, h]

        scores = lax.dot_general(qb, kb, (((1,), (1,)), ((), ())),
                                 preferred_element_type=F32)
        p = (scores * mask_ref[h]).astype(BF16)
        intra = jnp.dot(p, vb, preferred_element_type=F32)
        cross = jnp.dot((qr * qdec_ref[h]).astype(BF16), state.astype(BF16),
                        preferred_element_type=F32)
        o = intra + cross

        kd = (kr * kdec_ref[h]).astype(BF16)
        s_ref[0, h] = sdec_ref[h] * state + lax.dot_general(
            kd, vb, (((0,), (0,)), ((), ())), preferred_element_type=F32)

        mu = jnp.mean(o, axis=-1, keepdims=True)
        d = o - mu
        var = jnp.mean(d * d, axis=-1, keepdims=True)
        on = d * lax.rsqrt(var + EPS)
        o_ref[:, vs] = (_silu(gr_ref[:, vs]) * on).astype(BF16)


def _retention_tables(seq, ts, chunk, pos0):
    log_gamma = jnp.log(1.0 - 2.0 ** (-5.0 - jnp.arange(H_R, dtype=F32)))
    idx = jnp.arange(ts, dtype=F32)
    diff = idx[:, None] - idx[None, :]
    cn = (jnp.arange(ts) // chunk)[:, None]
    cm = (jnp.arange(ts) // chunk)[None, :]
    expo = jnp.where(cn == cm, jnp.abs(diff), diff)
    mask = jnp.where((cn >= cm)[None],
                     jnp.exp(log_gamma[:, None, None] * expo[None]), 0.0)
    ones = jnp.ones((1, 1, DK), F32)
    qdec = jnp.exp(log_gamma[:, None] * (idx[None, :] + 1.0))[..., None] * ones
    kdec = jnp.exp(log_gamma[:, None] * (ts - 1.0 - idx[None, :]))[..., None] * ones
    sdec = jnp.exp(log_gamma * ts)[:, None, None] * jnp.ones((1, 1, DV), F32)
    half = DK // 2
    inv_freq = 1.0 / (ROPE_BASE ** (jnp.arange(half, dtype=F32) / half))
    pos = (pos0 + jnp.arange(seq)).astype(F32)
    ang = pos[:, None] * inv_freq[None, :]
    cos = jnp.cos(ang)
    sin = jnp.sin(ang)
    cos2 = jnp.concatenate([cos, cos], axis=-1)
    sin2 = jnp.concatenate([-sin, sin], axis=-1)
    return cos2, sin2, mask, qdec, kdec, sdec


def _retention(z, s0, tables, row0, batch, seq, ts):
    cos2, sin2, mask, qdec, kdec, sdec = tables
    ns = seq // ts
    r0 = row0 // ts
    row = lambda b, s: r0 + b * ns + s
    const3 = lambda b, s: (0, 0, 0)
    return pl.pallas_call(
        _retention_kernel,
        grid=(batch, ns),
        in_specs=[
            pl.BlockSpec((ts, QKW), lambda b, s: (row(b, s), OFF_Q // QKW)),
            pl.BlockSpec((ts, QKW), lambda b, s: (row(b, s), OFF_K // QKW)),
            pl.BlockSpec((ts, RV), lambda b, s: (row(b, s), OFF_V // RV)),
            pl.BlockSpec((ts, RV), lambda b, s: (row(b, s), OFF_GR // RV)),
            pl.BlockSpec((ts, DK), lambda b, s: (s, 0)),
            pl.BlockSpec((ts, DK), lambda b, s: (s, 0)),
            pl.BlockSpec((H_R, ts, ts), const3),
            pl.BlockSpec((H_R, ts, DK), const3),
            pl.BlockSpec((H_R, ts, DK), const3),
            pl.BlockSpec((H_R, 1, DV), const3),
            pl.BlockSpec((1, H_R, DK, DV), lambda b, s: (b, 0, 0, 0)),
        ],
        out_specs=[
            pl.BlockSpec((ts, RV), lambda b, s: (b * ns + s, 0)),
            pl.BlockSpec((1, H_R, DK, DV), lambda b, s: (b, 0, 0, 0)),
        ],
        out_shape=[
            jax.ShapeDtypeStruct((batch * seq, RV), BF16),
            jax.ShapeDtypeStruct((batch, H_R, DK, DV), F32),
        ],
        compiler_params=_params("parallel", "arbitrary"),
        name="retention",
    )(z, z, z, z, cos2, sin2, mask, qdec, kdec, sdec, s0)


def _gmlp_kernel(u_ref, vm_ref, gm_ref, ws_ref, bias_ref, lng_ref, lnb_ref,
                 *rest, ts, lg, emit_vn):
    if emit_vn:
        m_ref, vn_ref, w2_scr = rest
    else:
        m_ref, w2_scr = rest

    @pl.when((pl.program_id(0) == 0) & (pl.program_id(1) == 0))
    def _():
        i = lax.broadcasted_iota(jnp.int32, (lg, lg), 0)
        j = lax.broadcasted_iota(jnp.int32, (lg, lg), 1)
        keep = (i // CHUNK) >= (j // CHUNK)
        w2_scr[...] = jnp.zeros_like(w2_scr)
        for g in range(G_MLP):
            wg = jnp.where(keep, ws_ref[g, :lg, :lg], 0.0).astype(BF16)
            for c in range(ts // lg):
                w2_scr[g, c * lg:(c + 1) * lg, c * lg:(c + 1) * lg] = wg

    gv = _gelu(vm_ref[...])
    mu = jnp.mean(gv, axis=-1, keepdims=True)
    d = gv - mu
    var = jnp.mean(d * d, axis=-1, keepdims=True)
    vn = d * lax.rsqrt(var + EPS) * lng_ref[0] + lnb_ref[0]
    if emit_vn:
        vn_ref[...] = vn
    for g in range(G_MLP):
        sl = slice(g * DG, (g + 1) * DG)
        s = jnp.dot(w2_scr[g], vn[:, sl].astype(BF16),
                    preferred_element_type=F32) + bias_ref[:, g:g + 1]
        m = _gelu(u_ref[:, sl]) * s
        m_ref[:, sl] = (_silu(gm_ref[:, sl]) * m).astype(BF16)


def _gmlp(z, ws, ws_b, ln_g, ln_b, layer, row0, batch, seq, ts, emit_vn):
    lg = min(GMLP_CHUNK, seq)
    ns = seq // ts
    t = batch * seq
    r0 = row0 // ts
    bias = jnp.tile(jnp.transpose(ws_b[layer, :, :lg]), (ts // lg, 1))
    row = lambda b, s: r0 + b * ns + s
    out_block = pl.BlockSpec((ts, D_MLP), lambda b, s: (b * ns + s, 0))
    out_specs = [out_block]
    out_shape = [jax.ShapeDtypeStruct((t, D_MLP), BF16)]
    if emit_vn:
        out_specs.append(out_block)
        out_shape.append(jax.ShapeDtypeStruct((t, D_MLP), F32))
    return pl.pallas_call(
        functools.partial(_gmlp_kernel, ts=ts, lg=lg, emit_vn=emit_vn),
        grid=(batch, ns),
        in_specs=[
            pl.BlockSpec((ts, D_MLP), lambda b, s: (row(b, s), OFF_U // D_MLP)),
            pl.BlockSpec((ts, D_MLP), lambda b, s: (row(b, s), OFF_VM // D_MLP)),
            pl.BlockSpec((ts, D_MLP), lambda b, s: (row(b, s), OFF_GM // D_MLP)),
            pl.BlockSpec((None, G_MLP, GMLP_CHUNK, GMLP_CHUNK),
                         lambda b, s: (layer, 0, 0, 0)),
            pl.BlockSpec((ts, G_MLP), lambda b, s: (0, 0)),
            pl.BlockSpec((1, 1, D_MLP), lambda b, s: (layer, 0, 0)),
            pl.BlockSpec((1, 1, D_MLP), lambda b, s: (layer, 0, 0)),
        ],
        out_specs=out_specs,
        out_shape=out_shape,
        scratch_shapes=[pltpu.VMEM((G_MLP, ts, ts), BF16)],
        compiler_params=_params("arbitrary", "arbitrary"),
        name="gmlp",
    )(z, z, z, ws, bias, ln_g, ln_b)


def _merge_out_kernel(rp_ref, rs_ref, mp_ref, ms_ref, ar_ref, am_ref, x_ref,
                      wr_ref, wm_ref, wo_ref, g_ref, *outs, final_norm,
                      first_block, prompt_blocks):
    is_prompt = first_block + pl.program_id(0) < prompt_blocks
    r = jnp.where(is_prompt, rp_ref[...], rs_ref[...])
    m = jnp.where(is_prompt, mp_ref[...], ms_ref[...])
    rb = jnp.dot(r, wr_ref[...], preferred_element_type=F32)
    mb = jnp.dot(m, wm_ref[...], preferred_element_type=F32)
    merged = jax.nn.sigmoid(ar_ref[...]) * rb + jax.nn.sigmoid(am_ref[...]) * mb
    y = x_ref[...] + jnp.dot(merged.astype(BF16), wo_ref[...],
                             preferred_element_type=F32)
    if final_norm:
        outs[0][...] = _rmsnorm(y, g_ref[0])
    else:
        outs[0][...] = y
        outs[1][...] = _rmsnorm(y, g_ref[0]).astype(BF16)


def _merge_out(r_p, r_s, m_p, m_s, z, x, w_ret, w_mlp, w_o, g, layer, g_layer,
               row0, rows, final_norm):
    tm = TM_OUT
    r0 = row0 // tm
    n_p = r_p.shape[0] // tm
    prompt_row = lambda i: (jnp.minimum(r0 + i, n_p - 1), 0)
    sample_row = lambda i: (jnp.maximum(r0 + i - n_p, 0), 0)
    resident = lambda: pl.BlockSpec((None, D_MODEL, D_MODEL),
                                    lambda i: (layer, 0, 0),
                                    pipeline_mode=pl.Buffered(1))
    out_block = pl.BlockSpec((tm, D_MODEL), lambda i: (i, 0))
    if final_norm:
        out_specs = out_block
        out_shape = jax.ShapeDtypeStruct((rows, D_MODEL), F32)
    else:
        out_specs = [out_block, out_block]
        out_shape = [jax.ShapeDtypeStruct((rows, D_MODEL), F32),
                     jax.ShapeDtypeStruct((rows, D_MODEL), BF16)]
    return pl.pallas_call(
        functools.partial(_merge_out_kernel, final_norm=final_norm,
                          first_block=r0, prompt_blocks=n_p),
        grid=(rows // tm,),
        in_specs=[
            pl.BlockSpec((tm, RV), prompt_row),
            pl.BlockSpec((tm, RV), sample_row),
            pl.BlockSpec((tm, D_MLP), prompt_row),
            pl.BlockSpec((tm, D_MLP), sample_row),
            pl.BlockSpec((tm, D_MODEL), lambda i: (r0 + i, OFF_AR // D_MODEL)),
            pl.BlockSpec((tm, D_MODEL), lambda i: (r0 + i, OFF_AM // D_MODEL)),
            pl.BlockSpec((tm, D_MODEL), lambda i: (r0 + i, 0)),
            resident(), resident(), resident(),
            pl.BlockSpec((1, 1, D_MODEL), lambda i: (g_layer, 0, 0)),
        ],
        out_specs=out_specs,
        out_shape=out_shape,
        compiler_params=_params("parallel"),
        name="merge_out",
    )(r_p, r_s, m_p, m_s, z, z, x, w_ret, w_mlp, w_o, g)


def kernel(x_prompt, x_sample, state_ret, norm_g, w_in, ws, ws_b, ln_g, ln_b,
           w_ret_out, w_mlp_out, w_o, final_g):
    b_p, s_p, _ = x_prompt.shape
    b_s, s_s, _ = x_sample.shape
    t_p = b_p * s_p
    t_s = b_s * s_s
    tab_p = _retention_tables(s_p, TS_RET, CHUNK, 0)
    tab_s = _retention_tables(s_s, s_s, s_s, PAST_LEN)
    zeros_state = jnp.zeros((b_p, H_R, DK, DV), F32)
    norm_g3 = norm_g.reshape(DEPTH, 1, D_MODEL)
    ln_g3 = ln_g.reshape(DEPTH, 1, D_MLP)
    ln_b3 = ln_b.reshape(DEPTH, 1, D_MLP)
    fg3 = final_g.reshape(1, 1, D_MODEL)
    w_ret_b = w_ret_out.astype(BF16)
    w_mlp_b = w_mlp_out.astype(BF16)
    w_o_b = w_o.astype(BF16)

    x = jnp.concatenate([x_prompt.reshape(t_p, D_MODEL),
                         x_sample.reshape(t_s, D_MODEL)], axis=0)
    h = _prenorm(x, norm_g3, 0)
    st_p, st_s, v_s = [], [], []
    for l in range(DEPTH):
        z = _in_proj(h, w_in, l)
        r_p, sp = _retention(z, zeros_state, tab_p, 0, b_p, s_p, TS_RET)
        r_s, ss = _retention(z, state_ret[l], tab_s, t_p, b_s, s_s, s_s)
        (m_p,) = _gmlp(z, ws, ws_b, ln_g3, ln_b3, l, 0, b_p, s_p, TS_MLP, False)
        m_s, vs = _gmlp(z, ws, ws_b, ln_g3, ln_b3, l, t_p, b_s, s_s, s_s, True)
        st_p.append(sp)
        st_s.append(ss)
        v_s.append(vs.reshape(b_s, s_s, D_MLP))
        merge = functools.partial(_merge_out, r_p, r_s, m_p, m_s, z, x, w_ret_b,
                                  w_mlp_b, w_o_b)
        if l < DEPTH - 1:
            x, h = merge(norm_g3, l, l + 1, 0, t_p + t_s, False)
        else:
            y_p = merge(fg3, l, 0, 0, t_p, True)
            y_s = merge(fg3, l, 0, t_p, t_s, True)
    return (y_p.reshape(b_p, s_p, D_MODEL), y_s.reshape(b_s, s_s, D_MODEL),
            jnp.stack(st_p), jnp.stack(st_s), jnp.stack(v_s))
```

```python
import functools

import jax
import jax.numpy as jnp
import numpy as np
from jax import lax
from jax.experimental import pallas as pl
from jax.experimental.pallas import tpu as pltpu

D_MODEL = 2048
DEPTH = 4
PAST_LEN = 1024
CHUNK = 64
H_R = 8
DK = D_MODEL // 16
DV = D_MODEL // 8
QKW = H_R * DK
RV = H_R * DV
D_MLP = D_MODEL
G_MLP = 8
DG = D_MLP // G_MLP
GMLP_CHUNK = 128
ROPE_BASE = 10000.0
EPS = 1e-6
N_IN = QKW * 2 + RV * 2 + D_MLP * 3 + D_MODEL * 2

OFF_Q = 0
OFF_K = OFF_Q + QKW
OFF_V = OFF_K + QKW
OFF_GR = OFF_V + RV
OFF_U = OFF_GR + RV
OFF_VM = OFF_U + D_MLP
OFF_GM = OFF_VM + D_MLP
OFF_AR = OFF_GM + D_MLP
OFF_AM = OFF_AR + D_MODEL

VMEM_LIMIT = 56 * 1024 * 1024

TM_IN = 1280
TN_IN = 1024
RC_IN = 256
TS_RET = 256
TS_MLP = 256
TM_OUT = 256

F32 = jnp.float32
BF16 = jnp.bfloat16


def _rmsnorm(x, g):
    return x * lax.rsqrt(jnp.mean(x * x, axis=-1, keepdims=True) + EPS) * g


def _params(*sem):
    return pltpu.CompilerParams(dimension_semantics=sem,
                                vmem_limit_bytes=VMEM_LIMIT)


def _prenorm_kernel(x_ref, g_ref, h_ref):
    h_ref[...] = _rmsnorm(x_ref[...], g_ref[0]).astype(BF16)


def _prenorm(x, norm_g, layer):
    t = x.shape[0]
    return pl.pallas_call(
        _prenorm_kernel,
        grid=(t // TM_IN,),
        in_specs=[
            pl.BlockSpec((TM_IN, D_MODEL), lambda i: (i, 0)),
            pl.BlockSpec((1, 1, D_MODEL), lambda i: (layer, 0, 0)),
        ],
        out_specs=pl.BlockSpec((TM_IN, D_MODEL), lambda i: (i, 0)),
        out_shape=jax.ShapeDtypeStruct((t, D_MODEL), BF16),
        compiler_params=_params("parallel"),
        name="prenorm",
    )(x, norm_g)


def _cast_weight_tile(w_ref, wb_scr):
    @pl.when(pl.program_id(1) == 0)
    def _():
        wb_scr[...] = w_ref[...].astype(BF16)


def _in_proj_qk_kernel(h_ref, w_ref, cos_ref, sin_ref, z_ref, wb_scr):
    _cast_weight_tile(w_ref, wb_scr)
    scale = jnp.where(pl.program_id(0) == 1, np.float32(DK ** -0.5), np.float32(1.0))
    for c in range(TM_IN // RC_IN):
        rows = slice(c * RC_IN, (c + 1) * RC_IN)
        x = jnp.dot(h_ref[rows, :], wb_scr[...], preferred_element_type=F32)
        cos = cos_ref[rows, :]
        sin = sin_ref[rows, :]
        for h in range(TN_IN // DK):
            ks = slice(h * DK, (h + 1) * DK)
            xs = x[:, ks]
            r = xs * cos + pltpu.roll(xs, DK // 2, 1) * sin
            z_ref[rows, ks] = (r * scale).astype(BF16)


def _activation(act, x):
    if act == "copy":
        return x
    if act == "sigmoid":
        return jax.nn.sigmoid(x)
    if act == "silu":
        return x * jax.nn.sigmoid(x)
    if act == "gelu":
        return 0.5 * x * (1.0 + lax.erf(x * np.float32(np.sqrt(0.5))))
    raise ValueError(act)


def _in_proj_act_kernel(h_ref, w_ref, z_ref, wb_scr, *, act):
    _cast_weight_tile(w_ref, wb_scr)
    for c in range(TM_IN // RC_IN):
        rows = slice(c * RC_IN, (c + 1) * RC_IN)
        x = jnp.dot(h_ref[rows, :], wb_scr[...], preferred_element_type=F32)
        z_ref[rows, :] = _activation(act, x).astype(BF16)


def _in_proj_qk(h, w_in, layer, cos_rows, sin_rows):
    t = h.shape[0]
    return pl.pallas_call(
        _in_proj_qk_kernel,
        grid=(2 * QKW // TN_IN, t // TM_IN),
        in_specs=[
            pl.BlockSpec((TM_IN, D_MODEL), lambda j, i: (i, 0)),
            pl.BlockSpec((None, D_MODEL, TN_IN), lambda j, i: (layer, 0, j)),
            pl.BlockSpec((TM_IN, DK), lambda j, i: (i, 0)),
            pl.BlockSpec((TM_IN, DK), lambda j, i: (i, 0)),
        ],
        out_specs=pl.BlockSpec((TM_IN, TN_IN), lambda j, i: (i, j)),
        out_shape=jax.ShapeDtypeStruct((t, 2 * QKW), BF16),
        scratch_shapes=[pltpu.VMEM((D_MODEL, TN_IN), BF16)],
        compiler_params=_params("arbitrary", "arbitrary"),
        name="in_proj_qk",
    )(h, w_in, cos_rows, sin_rows)


def _in_proj_act(h, w_in, layer, offsets, width, act):
    t = h.shape[0]
    per = width // TN_IN

    def src_tile(j):
        tile = offsets[0] // TN_IN
        for n, off in enumerate(offsets[1:], start=1):
            tile = jnp.where(j // per == n, off // TN_IN, tile)
        return tile + j % per

    return pl.pallas_call(
        functools.partial(_in_proj_act_kernel, act=act),
        grid=(len(offsets) * per, t // TM_IN),
        in_specs=[
            pl.BlockSpec((TM_IN, D_MODEL), lambda j, i: (i, 0)),
            pl.BlockSpec((None, D_MODEL, TN_IN), lambda j, i: (layer, 0, src_tile(j))),
        ],
        out_specs=pl.BlockSpec((TM_IN, TN_IN), lambda j, i: (i, j)),
        out_shape=jax.ShapeDtypeStruct((t, len(offsets) * width), BF16),
        scratch_shapes=[pltpu.VMEM((D_MODEL, TN_IN), BF16)],
        compiler_params=_params("arbitrary", "arbitrary"),
        name="in_proj_" + act,
    )(h, w_in)


def _retention_kernel(q_ref, k_ref, v_ref, gr_ref, mask_ref, qdec_ref, kdec_ref,
                      sdec_ref, s0_ref, o_ref, s_ref):
    @pl.when(pl.program_id(1) == 0)
    def _():
        s_ref[...] = s0_ref[...]

    for h in range(H_R):
        ks = slice(h * DK, (h + 1) * DK)
        vs = slice(h * DV, (h + 1) * DV)
        qb = q_ref[:, ks]
        kb = k_ref[:, ks]
        vb = v_ref[:, vs]
        state = s_ref[0, h]

        scores = lax.dot_general(qb, kb, (((1,), (1,)), ((), ())),
                                 preferred_element_type=F32)
        p = (scores * mask_ref[h]).astype(BF16)
        intra = jnp.dot(p, vb, preferred_element_type=F32)
        qd = (qb.astype(F32) * qdec_ref[h]).astype(BF16)
        cross = jnp.dot(qd, state.astype(BF16), preferred_element_type=F32)
        o = intra + cross

        kd = (kb.astype(F32) * kdec_ref[h]).astype(BF16)
        s_ref[0, h] = sdec_ref[h] * state + lax.dot_general(
            kd, vb, (((0,), (0,)), ((), ())), preferred_element_type=F32)

        mu = jnp.mean(o, axis=-1, keepdims=True)
        d = o - mu
        var = jnp.mean(d * d, axis=-1, keepdims=True)
        on = d * lax.rsqrt(var + EPS)
        o_ref[:, vs] = (gr_ref[:, vs].astype(F32) * on).astype(BF16)


def _retention_tables(ts, chunk):
    log_gamma = jnp.log(1.0 - 2.0 ** (-5.0 - jnp.arange(H_R, dtype=F32)))
    idx = jnp.arange(ts, dtype=F32)
    diff = idx[:, None] - idx[None, :]
    cn = (jnp.arange(ts) // chunk)[:, None]
    cm = (jnp.arange(ts) // chunk)[None, :]
    expo = jnp.where(cn == cm, jnp.abs(diff), diff)
    mask = jnp.where((cn >= cm)[None],
                     jnp.exp(log_gamma[:, None, None] * expo[None]), 0.0)
    ones = jnp.ones((1, 1, DK), F32)
    qdec = jnp.exp(log_gamma[:, None] * (idx[None, :] + 1.0))[..., None] * ones
    kdec = jnp.exp(log_gamma[:, None] * (ts - 1.0 - idx[None, :]))[..., None] * ones
    sdec = jnp.exp(log_gamma * ts)[:, None, None] * jnp.ones((1, 1, DV), F32)
    return mask, qdec, kdec, sdec


def _rotary_tables(seq, pos0):
    half = DK // 2
    inv_freq = 1.0 / (ROPE_BASE ** (jnp.arange(half, dtype=F32) / half))
    pos = (pos0 + jnp.arange(seq)).astype(F32)
    ang = pos[:, None] * inv_freq[None, :]
    cos = jnp.cos(ang)
    sin = jnp.sin(ang)
    return (jnp.concatenate([cos, cos], axis=-1),
            jnp.concatenate([-sin, sin], axis=-1))


def _retention(qk, v, gates, s0, tables, row0, batch, seq, ts):
    mask, qdec, kdec, sdec = tables
    ns = seq // ts
    r0 = row0 // ts
    row = lambda b, s: r0 + b * ns + s
    const3 = lambda b, s: (0, 0, 0)
    return pl.pallas_call(
        _retention_kernel,
        grid=(batch, ns),
        in_specs=[
            pl.BlockSpec((ts, QKW), lambda b, s: (row(b, s), 0)),
            pl.BlockSpec((ts, QKW), lambda b, s: (row(b, s), 1)),
            pl.BlockSpec((ts, RV), lambda b, s: (row(b, s), 0)),
            pl.BlockSpec((ts, RV), lambda b, s: (row(b, s), 0)),
            pl.BlockSpec((H_R, ts, ts), const3),
            pl.BlockSpec((H_R, ts, DK), const3),
            pl.BlockSpec((H_R, ts, DK), const3),
            pl.BlockSpec((H_R, 1, DV), const3),
            pl.BlockSpec((1, H_R, DK, DV), lambda b, s: (b, 0, 0, 0)),
        ],
        out_specs=[
            pl.BlockSpec((ts, RV), lambda b, s: (b * ns + s, 0)),
            pl.BlockSpec((1, H_R, DK, DV), lambda b, s: (b, 0, 0, 0)),
        ],
        out_shape=[
            jax.ShapeDtypeStruct((batch * seq, RV), BF16),
            jax.ShapeDtypeStruct((batch, H_R, DK, DV), F32),
        ],
        compiler_params=_params("parallel", "arbitrary"),
        name="retention",
    )(qk, qk, v, gates, mask, qdec, kdec, sdec, s0)


def _gmlp_kernel(u_ref, vm_ref, gm_ref, ws_ref, bias_ref, lng_ref, lnb_ref,
                 *rest, ts, lg, emit_vn):
    if emit_vn:
        m_ref, vn_ref, w2_scr = rest
    else:
        m_ref, w2_scr = rest

    @pl.when((pl.program_id(0) == 0) & (pl.program_id(1) == 0))
    def _():
        i = lax.broadcasted_iota(jnp.int32, (lg, lg), 0)
        j = lax.broadcasted_iota(jnp.int32, (lg, lg), 1)
        keep = (i // CHUNK) >= (j // CHUNK)
        w2_scr[...] = jnp.zeros_like(w2_scr)
        for g in range(G_MLP):
            wg = jnp.where(keep, ws_ref[g, :lg, :lg], 0.0).astype(BF16)
            for c in range(ts // lg):
                w2_scr[g, c * lg:(c + 1) * lg, c * lg:(c + 1) * lg] = wg

    gv = vm_ref[...].astype(F32)
    mu = jnp.mean(gv, axis=-1, keepdims=True)
    d = gv - mu
    var = jnp.mean(d * d, axis=-1, keepdims=True)
    vn = d * lax.rsqrt(var + EPS) * lng_ref[0] + lnb_ref[0]
    if emit_vn:
        vn_ref[...] = vn
    for g in range(G_MLP):
        sl = slice(g * DG, (g + 1) * DG)
        s = jnp.dot(w2_scr[g], vn[:, sl].astype(BF16),
                    preferred_element_type=F32) + bias_ref[:, g:g + 1]
        m = u_ref[:, sl].astype(F32) * s
        m_ref[:, sl] = (gm_ref[:, sl].astype(F32) * m).astype(BF16)


def _gmlp(uv, gates, ws, ws_b, ln_g, ln_b, layer, row0, batch, seq, ts, emit_vn):
    lg = min(GMLP_CHUNK, seq)
    ns = seq // ts
    t = batch * seq
    r0 = row0 // ts
    bias = jnp.tile(jnp.transpose(ws_b[layer, :, :lg]), (ts // lg, 1))
    row = lambda b, s: r0 + b * ns + s
    out_block = pl.BlockSpec((ts, D_MLP), lambda b, s: (b * ns + s, 0))
    out_specs = [out_block]
    out_shape = [jax.ShapeDtypeStruct((t, D_MLP), BF16)]
    if emit_vn:
        out_specs.append(out_block)
        out_shape.append(jax.ShapeDtypeStruct((t, D_MLP), F32))
    return pl.pallas_call(
        functools.partial(_gmlp_kernel, ts=ts, lg=lg, emit_vn=emit_vn),
        grid=(batch, ns),
        in_specs=[
            pl.BlockSpec((ts, D_MLP), lambda b, s: (row(b, s), 0)),
            pl.BlockSpec((ts, D_MLP), lambda b, s: (row(b, s), 1)),
            pl.BlockSpec((ts, D_MLP), lambda b, s: (row(b, s), 1)),
            pl.BlockSpec((None, G_MLP, GMLP_CHUNK, GMLP_CHUNK),
                         lambda b, s: (layer, 0, 0, 0)),
            pl.BlockSpec((ts, G_MLP), lambda b, s: (0, 0)),
            pl.BlockSpec((1, 1, D_MLP), lambda b, s: (layer, 0, 0)),
            pl.BlockSpec((1, 1, D_MLP), lambda b, s: (layer, 0, 0)),
        ],
        out_specs=out_specs,
        out_shape=out_shape,
        scratch_shapes=[pltpu.VMEM((G_MLP, ts, ts), BF16)],
        compiler_params=_params("arbitrary", "arbitrary"),
        name="gmlp",
    )(uv, uv, gates, ws, bias, ln_g, ln_b)


def _merge_out_kernel(rp_ref, rs_ref, mp_ref, ms_ref, ar_ref, am_ref, x_ref,
                      wr_ref, wm_ref, wo_ref, g_ref, *outs, final_norm,
                      first_block, prompt_blocks):
    is_prompt = first_block + pl.program_id(0) < prompt_blocks
    r = jnp.where(is_prompt, rp_ref[...], rs_ref[...])
    m = jnp.where(is_prompt, mp_ref[...], ms_ref[...])
    rb = jnp.dot(r, wr_ref[...], preferred_element_type=F32)
    mb = jnp.dot(m, wm_ref[...], preferred_element_type=F32)
    merged = ar_ref[...].astype(F32) * rb + am_ref[...].astype(F32) * mb
    y = x_ref[...] + jnp.dot(merged.astype(BF16), wo_ref[...],
                             preferred_element_type=F32)
    if final_norm:
        outs[0][...] = _rmsnorm(y, g_ref[0])
    else:
        outs[0][...] = y
        outs[1][...] = _rmsnorm(y, g_ref[0]).astype(BF16)


def _merge_out(r_p, r_s, m_p, m_s, a_sig, x, w_ret, w_mlp, w_o, g, layer, g_layer,
               row0, rows, final_norm):
    tm = TM_OUT
    r0 = row0 // tm
    n_p = r_p.shape[0] // tm
    prompt_row = lambda i: (jnp.minimum(r0 + i, n_p - 1), 0)
    sample_row = lambda i: (jnp.maximum(r0 + i - n_p, 0), 0)
    resident = lambda: pl.BlockSpec((None, D_MODEL, D_MODEL),
                                    lambda i: (layer, 0, 0),
                                    pipeline_mode=pl.Buffered(1))
    out_block = pl.BlockSpec((tm, D_MODEL), lambda i: (i, 0))
    if final_norm:
        out_specs = out_block
        out_shape = jax.ShapeDtypeStruct((rows, D_MODEL), F32)
    else:
        out_specs = [out_block, out_block]
        out_shape = [jax.ShapeDtypeStruct((rows, D_MODEL), F32),
                     jax.ShapeDtypeStruct((rows, D_MODEL), BF16)]
    return pl.pallas_call(
        functools.partial(_merge_out_kernel, final_norm=final_norm,
                          first_block=r0, prompt_blocks=n_p),
        grid=(rows // tm,),
        in_specs=[
            pl.BlockSpec((tm, RV), prompt_row),
            pl.BlockSpec((tm, RV), sample_row),
            pl.BlockSpec((tm, D_MLP), prompt_row),
            pl.BlockSpec((tm, D_MLP), sample_row),
            pl.BlockSpec((tm, D_MODEL), lambda i: (r0 + i, 0)),
            pl.BlockSpec((tm, D_MODEL), lambda i: (r0 + i, 1)),
            pl.BlockSpec((tm, D_MODEL), lambda i: (r0 + i, 0)),
            resident(), resident(), resident(),
            pl.BlockSpec((1, 1, D_MODEL), lambda i: (g_layer, 0, 0)),
        ],
        out_specs=out_specs,
        out_shape=out_shape,
        compiler_params=_params("parallel"),
        name="merge_out",
    )(r_p, r_s, m_p, m_s, a_sig, a_sig, x, w_ret, w_mlp, w_o, g)


def kernel(x_prompt, x_sample, state_ret, norm_g, w_in, ws, ws_b, ln_g, ln_b,
           w_ret_out, w_mlp_out, w_o, final_g):
    b_p, s_p, _ = x_prompt.shape
    b_s, s_s, _ = x_sample.shape
    t_p = b_p * s_p
    t_s = b_s * s_s
    tab_p = _retention_tables(TS_RET, CHUNK)
    tab_s = _retention_tables(s_s, s_s)
    cos_p, sin_p = _rotary_tables(s_p, 0)
    cos_s, sin_s = _rotary_tables(s_s, PAST_LEN)
    cos_rows = jnp.concatenate([jnp.tile(cos_p, (b_p, 1)), jnp.tile(cos_s, (b_s, 1))])
    sin_rows = jnp.concatenate([jnp.tile(sin_p, (b_p, 1)), jnp.tile(sin_s, (b_s, 1))])
    zeros_state = jnp.zeros((b_p, H_R, DK, DV), F32)
    norm_g3 = norm_g.reshape(DEPTH, 1, D_MODEL)
    ln_g3 = ln_g.reshape(DEPTH, 1, D_MLP)
    ln_b3 = ln_b.reshape(DEPTH, 1, D_MLP)
    fg3 = final_g.reshape(1, 1, D_MODEL)
    w_ret_b = w_ret_out.astype(BF16)
    w_mlp_b = w_mlp_out.astype(BF16)
    w_o_b = w_o.astype(BF16)

    x = jnp.concatenate([x_prompt.reshape(t_p, D_MODEL),
                         x_sample.reshape(t_s, D_MODEL)], axis=0)
    h = _prenorm(x, norm_g3, 0)
    st_p, st_s, v_s = [], [], []
    for l in range(DEPTH):
        qk = _in_proj_qk(h, w_in, l, cos_rows, sin_rows)
        v = _in_proj_act(h, w_in, l, (OFF_V,), RV, "copy")
        gates = _in_proj_act(h, w_in, l, (OFF_GR, OFF_GM), RV, "silu")
        uv = _in_proj_act(h, w_in, l, (OFF_U, OFF_VM), D_MLP, "gelu")
        a_sig = _in_proj_act(h, w_in, l, (OFF_AR, OFF_AM), D_MODEL, "sigmoid")
        r_p, sp = _retention(qk, v, gates, zeros_state, tab_p, 0, b_p, s_p, TS_RET)
        r_s, ss = _retention(qk, v, gates, state_ret[l], tab_s, t_p, b_s, s_s, s_s)
        (m_p,) = _gmlp(uv, gates, ws, ws_b, ln_g3, ln_b3, l, 0, b_p, s_p, TS_MLP, False)
        m_s, vs = _gmlp(uv, gates, ws, ws_b, ln_g3, ln_b3, l, t_p, b_s, s_s, s_s, True)
        st_p.append(sp)
        st_s.append(ss)
        v_s.append(vs.reshape(b_s, s_s, D_MLP))
        merge = functools.partial(_merge_out, r_p, r_s, m_p, m_s, a_sig, x, w_ret_b,
                                  w_mlp_b, w_o_b)
        if l < DEPTH - 1:
            x, h = merge(norm_g3, l, l + 1, 0, t_p + t_s, False)
        else:
            y_p = merge(fg3, l, 0, 0, t_p, True)
            y_s = merge(fg3, l, 0, t_p, t_s, True)
    return (y_p.reshape(b_p, s_p, D_MODEL), y_s.reshape(b_s, s_s, D_MODEL),
            jnp.stack(st_p), jnp.stack(st_s), jnp.stack(v_s))
```

```python
import functools

import jax
import jax.numpy as jnp
import numpy as np
from jax import lax
from jax.experimental import pallas as pl
from jax.experimental.pallas import tpu as pltpu

D_MODEL = 2048
DEPTH = 4
PAST_LEN = 1024
CHUNK = 64
H_R = 8
DK = D_MODEL // 16
DV = D_MODEL // 8
QKW = H_R * DK
RV = H_R * DV
D_MLP = D_MODEL
G_MLP = 8
DG = D_MLP // G_MLP
GMLP_CHUNK = 128
ROPE_BASE = 10000.0
EPS = 1e-6
N_IN = QKW * 2 + RV * 2 + D_MLP * 3 + D_MODEL * 2

OFF_Q = 0
OFF_K = OFF_Q + QKW
OFF_V = OFF_K + QKW
OFF_GR = OFF_V + RV
OFF_U = OFF_GR + RV
OFF_VM = OFF_U + D_MLP
OFF_GM = OFF_VM + D_MLP
OFF_AR = OFF_GM + D_MLP
OFF_AM = OFF_AR + D_MODEL

VMEM_LIMIT = 56 * 1024 * 1024

TM_IN = 1280
TN_IN = 1024
RC = 256
TS_MIX = 256
TM_OUT = 512

F32 = jnp.float32
BF16 = jnp.bfloat16


def _rmsnorm(x, g):
    return x * lax.rsqrt(jnp.mean(x * x, axis=-1, keepdims=True) + EPS) * g


def _params(*sem):
    return pltpu.CompilerParams(dimension_semantics=sem,
                                vmem_limit_bytes=VMEM_LIMIT)


def _resident(block_shape, index_map):
    return pl.BlockSpec(block_shape, index_map, pipeline_mode=pl.Buffered(1))


_ANY = pl.BlockSpec(memory_space=pl.ANY)


def _prenorm_kernel(x_ref, g_ref, *rest):
    h_ref = rest[-1]
    h_ref[...] = _rmsnorm(x_ref[...], g_ref[0]).astype(BF16)


def _prenorm(x, norm_g, layer, tm, total_rows, row0, dest=None):
    r0 = row0 // tm
    in_specs = [
        pl.BlockSpec((tm, D_MODEL), lambda i: (i, 0)),
        pl.BlockSpec((1, 1, D_MODEL), lambda i: (layer, 0, 0)),
    ]
    args = [x, norm_g]
    aliases = {}
    if dest is not None:
        in_specs.append(_ANY)
        args.append(dest)
        aliases = {2: 0}
    return pl.pallas_call(
        _prenorm_kernel,
        grid=(x.shape[0] // tm,),
        in_specs=in_specs,
        out_specs=pl.BlockSpec((tm, D_MODEL), lambda i: (r0 + i, 0)),
        out_shape=jax.ShapeDtypeStruct((total_rows, D_MODEL), BF16),
        input_output_aliases=aliases,
        compiler_params=_params("parallel"),
        name="prenorm",
    )(*args)


def _cast_weight_tile(w_ref, wb_scr):
    @pl.when(pl.program_id(1) == 0)
    def _():
        wb_scr[...] = w_ref[...].astype(BF16)


def _in_proj_qk_kernel(h_ref, w_ref, cos_ref, sin_ref, z_ref, wb_scr):
    _cast_weight_tile(w_ref, wb_scr)
    scale = jnp.where(pl.program_id(0) == 1, np.float32(DK ** -0.5), np.float32(1.0))
    for c in range(TM_IN // RC):
        rows = slice(c * RC, (c + 1) * RC)
        x = jnp.dot(h_ref[rows, :], wb_scr[...], preferred_element_type=F32)
        cos = cos_ref[rows, :]
        sin = sin_ref[rows, :]
        for h in range(TN_IN // DK):
            ks = slice(h * DK, (h + 1) * DK)
            xs = x[:, ks]
            r = xs * cos + pltpu.roll(xs, DK // 2, 1) * sin
            z_ref[rows, ks] = (r * scale).astype(BF16)


def _activation(act, x):
    if act == "copy":
        return x
    if act == "sigmoid":
        return jax.nn.sigmoid(x)
    if act == "silu":
        return x * jax.nn.sigmoid(x)
    if act == "gelu":
        return 0.5 * x * (1.0 + lax.erf(x * np.float32(np.sqrt(0.5))))
    raise ValueError(act)


def _in_proj_act_kernel(h_ref, w_ref, z_ref, wb_scr, *, act):
    _cast_weight_tile(w_ref, wb_scr)
    for c in range(TM_IN // RC):
        rows = slice(c * RC, (c + 1) * RC)
        x = jnp.dot(h_ref[rows, :], wb_scr[...], preferred_element_type=F32)
        z_ref[rows, :] = _activation(act, x).astype(BF16)


def _in_proj_qk(h, w_in, layer, cos_rows, sin_rows):
    t = h.shape[0]
    return pl.pallas_call(
        _in_proj_qk_kernel,
        grid=(2 * QKW // TN_IN, t // TM_IN),
        in_specs=[
            pl.BlockSpec((TM_IN, D_MODEL), lambda j, i: (i, 0)),
            pl.BlockSpec((None, D_MODEL, TN_IN), lambda j, i: (layer, 0, j)),
            pl.BlockSpec((TM_IN, DK), lambda j, i: (i, 0)),
            pl.BlockSpec((TM_IN, DK), lambda j, i: (i, 0)),
        ],
        out_specs=pl.BlockSpec((TM_IN, TN_IN), lambda j, i: (i, j)),
        out_shape=jax.ShapeDtypeStruct((t, 2 * QKW), BF16),
        scratch_shapes=[pltpu.VMEM((D_MODEL, TN_IN), BF16)],
        compiler_params=_params("arbitrary", "arbitrary"),
        name="in_proj_qk",
    )(h, w_in, cos_rows, sin_rows)


def _in_proj_act(h, w_in, layer, offsets, width, act):
    t = h.shape[0]
    per = width // TN_IN

    def src_tile(j):
        tile = offsets[0] // TN_IN
        for n, off in enumerate(offsets[1:], start=1):
            tile = jnp.where(j // per == n, off // TN_IN, tile)
        return tile + j % per

    return pl.pallas_call(
        functools.partial(_in_proj_act_kernel, act=act),
        grid=(len(offsets) * per, t // TM_IN),
        in_specs=[
            pl.BlockSpec((TM_IN, D_MODEL), lambda j, i: (i, 0)),
            pl.BlockSpec((None, D_MODEL, TN_IN), lambda j, i: (layer, 0, src_tile(j))),
        ],
        out_specs=pl.BlockSpec((TM_IN, TN_IN), lambda j, i: (i, j)),
        out_shape=jax.ShapeDtypeStruct((t, len(offsets) * width), BF16),
        scratch_shapes=[pltpu.VMEM((D_MODEL, TN_IN), BF16)],
        compiler_params=_params("arbitrary", "arbitrary"),
        name="in_proj_" + act,
    )(h, w_in)


def _retention_tile(q_ref, k_ref, v_ref, gr_ref, mask_ref, qdec_ref, kdec_ref,
                    sdec_ref, s_ref, o_ref):
    for h in range(H_R):
        ks = slice(h * DK, (h + 1) * DK)
        vs = slice(h * DV, (h + 1) * DV)
        qb = q_ref[:, ks]
        kb = k_ref[:, ks]
        vb = v_ref[:, vs]
        state = s_ref[0, h]

        scores = lax.dot_general(qb, kb, (((1,), (1,)), ((), ())),
                                 preferred_element_type=F32)
        p = (scores * mask_ref[h]).astype(BF16)
        intra = jnp.dot(p, vb, preferred_element_type=F32)
        qd = (qb.astype(F32) * qdec_ref[h]).astype(BF16)
        cross = jnp.dot(qd, state.astype(BF16), preferred_element_type=F32)
        o = intra + cross

        kd = (kb.astype(F32) * kdec_ref[h]).astype(BF16)
        s_ref[0, h] = sdec_ref[h] * state + lax.dot_general(
            kd, vb, (((0,), (0,)), ((), ())), preferred_element_type=F32)

        mu = jnp.mean(o, axis=-1, keepdims=True)
        d = o - mu
        var = jnp.mean(d * d, axis=-1, keepdims=True)
        on = d * lax.rsqrt(var + EPS)
        o_ref[:, vs] = (gr_ref[:, vs].astype(F32) * on).astype(BF16)


def _build_mixing_matrix(ws_ref, w2_scr, ts, lg):
    i = lax.broadcasted_iota(jnp.int32, (lg, lg), 0)
    j = lax.broadcasted_iota(jnp.int32, (lg, lg), 1)
    keep = (i // CHUNK) >= (j // CHUNK)
    w2_scr[...] = jnp.zeros_like(w2_scr)
    for g in range(G_MLP):
        wg = jnp.where(keep, ws_ref[g, :lg, :lg], 0.0).astype(BF16)
        for c in range(ts // lg):
            w2_scr[g, c * lg:(c + 1) * lg, c * lg:(c + 1) * lg] = wg


def _gmlp_tile(u_ref, vm_ref, gm_ref, bias_ref, lng_ref, lnb_ref, w2_scr, m_ref,
               vn_ref=None):
    gv = vm_ref[...].astype(F32)
    mu = jnp.mean(gv, axis=-1, keepdims=True)
    d = gv - mu
    var = jnp.mean(d * d, axis=-1, keepdims=True)
    vn = d * lax.rsqrt(var + EPS) * lng_ref[0] + lnb_ref[0]
    if vn_ref is not None:
        vn_ref[...] = vn
    for g in range(G_MLP):
        sl = slice(g * DG, (g + 1) * DG)
        s = jnp.dot(w2_scr[g], vn[:, sl].astype(BF16),
                    preferred_element_type=F32) + bias_ref[:, g:g + 1]
        m = u_ref[:, sl].astype(F32) * s
        m_ref[:, sl] = (gm_ref[:, sl].astype(F32) * m).astype(BF16)


def _retention_tables(ts, chunk):
    log_gamma = jnp.log(1.0 - 2.0 ** (-5.0 - jnp.arange(H_R, dtype=F32)))
    idx = jnp.arange(ts, dtype=F32)
    diff = idx[:, None] - idx[None, :]
    cn = (jnp.arange(ts) // chunk)[:, None]
    cm = (jnp.arange(ts) // chunk)[None, :]
    expo = jnp.where(cn == cm, jnp.abs(diff), diff)
    mask = jnp.where((cn >= cm)[None],
                     jnp.exp(log_gamma[:, None, None] * expo[None]), 0.0)
    ones = jnp.ones((1, 1, DK), F32)
    qdec = jnp.exp(log_gamma[:, None] * (idx[None, :] + 1.0))[..., None] * ones
    kdec = jnp.exp(log_gamma[:, None] * (ts - 1.0 - idx[None, :]))[..., None] * ones
    sdec = jnp.exp(log_gamma * ts)[:, None, None] * jnp.ones((1, 1, DV), F32)
    return mask, qdec, kdec, sdec


def _rotary_tables(seq, pos0):
    half = DK // 2
    inv_freq = 1.0 / (ROPE_BASE ** (jnp.arange(half, dtype=F32) / half))
    pos = (pos0 + jnp.arange(seq)).astype(F32)
    ang = pos[:, None] * inv_freq[None, :]
    cos = jnp.cos(ang)
    sin = jnp.sin(ang)
    return (jnp.concatenate([cos, cos], axis=-1),
            jnp.concatenate([-sin, sin], axis=-1))


def _mixing_bias(ws_b, layer, ts, lg):
    return jnp.tile(jnp.transpose(ws_b[layer, :, :lg]), (ts // lg, 1))


def _mixer_kernel(q_ref, k_ref, v_ref, gr_ref, gm_ref, u_ref, vm_ref, ar_ref,
                  am_ref, mask_ref, qdec_ref, kdec_ref, sdec_ref, ws_ref,
                  bias_ref, lng_ref, lnb_ref, wr_ref, wm_ref,
                  merged_ref, s_ref, w2_scr, r_scr, m_scr, *, ts, lg):
    @pl.when((pl.program_id(0) == 0) & (pl.program_id(1) == 0))
    def _():
        _build_mixing_matrix(ws_ref, w2_scr, ts, lg)

    @pl.when(pl.program_id(1) == 0)
    def _():
        s_ref[...] = jnp.zeros_like(s_ref)

    _gmlp_tile(u_ref, vm_ref, gm_ref, bias_ref, lng_ref, lnb_ref, w2_scr, m_scr)
    _retention_tile(q_ref, k_ref, v_ref, gr_ref, mask_ref, qdec_ref, kdec_ref,
                    sdec_ref, s_ref, r_scr)
    mb = jnp.dot(m_scr[...], wm_ref[...], preferred_element_type=F32)
    rb = jnp.dot(r_scr[...], wr_ref[...], preferred_element_type=F32)
    merged = ar_ref[...].astype(F32) * rb + am_ref[...].astype(F32) * mb
    merged_ref[...] = merged.astype(BF16)


def _mixer(qk, v, gates, uv, a_sig, tables, ws, ws_b, ln_g, ln_b, w_ret, w_mlp,
           layer, batch, seq):
    mask, qdec, kdec, sdec = tables
    ts = TS_MIX
    lg = GMLP_CHUNK
    ns = seq // ts
    row = lambda b, s: b * ns + s
    const3 = lambda b, s: (0, 0, 0)
    bias = _mixing_bias(ws_b, layer, ts, lg)
    tile = lambda width, col: pl.BlockSpec((ts, width), lambda b, s: (row(b, s), col))
    return pl.pallas_call(
        functools.partial(_mixer_kernel, ts=ts, lg=lg),
        grid=(batch, ns),
        in_specs=[
            tile(QKW, 0), tile(QKW, 1),
            tile(RV, 0),
            tile(RV, 0), tile(D_MLP, 1),
            tile(D_MLP, 0), tile(D_MLP, 1),
            tile(D_MODEL, 0), tile(D_MODEL, 1),
            _resident((H_R, ts, ts), const3),
            _resident((H_R, ts, DK), const3),
            _resident((H_R, ts, DK), const3),
            _resident((H_R, 1, DV), const3),
            _resident((None, G_MLP, GMLP_CHUNK, GMLP_CHUNK),
                      lambda b, s: (layer, 0, 0, 0)),
            _resident((ts, G_MLP), lambda b, s: (0, 0)),
            _resident((1, 1, D_MLP), lambda b, s: (layer, 0, 0)),
            _resident((1, 1, D_MLP), lambda b, s: (layer, 0, 0)),
            _resident((None, RV, D_MODEL), lambda b, s: (layer, 0, 0)),
            _resident((None, D_MLP, D_MODEL), lambda b, s: (layer, 0, 0)),
        ],
        out_specs=[
            pl.BlockSpec((ts, D_MODEL), lambda b, s: (row(b, s), 0)),
            pl.BlockSpec((1, H_R, DK, DV), lambda b, s: (b, 0, 0, 0)),
        ],
        out_shape=[
            jax.ShapeDtypeStruct((batch * seq, D_MODEL), BF16),
            jax.ShapeDtypeStruct((batch, H_R, DK, DV), F32),
        ],
        scratch_shapes=[pltpu.VMEM((G_MLP, ts, ts), BF16),
                        pltpu.VMEM((ts, RV), BF16),
                        pltpu.VMEM((ts, D_MLP), BF16)],
        compiler_params=_params("arbitrary", "arbitrary"),
        name="mixer",
    )(qk, qk, v, gates, gates, uv, uv, a_sig, a_sig, mask, qdec, kdec, sdec,
      ws, bias, ln_g, ln_b, w_ret, w_mlp)


def _out_proj_kernel(mg_ref, x_ref, wo_ref, g_ref, *outs, final_norm):
    for c in range(TM_OUT // RC):
        rows = slice(c * RC, (c + 1) * RC)
        y = x_ref[rows, :] + jnp.dot(mg_ref[rows, :], wo_ref[...],
                                     preferred_element_type=F32)
        if final_norm:
            outs[0][rows, :] = _rmsnorm(y, g_ref[0])
        else:
            outs[0][rows, :] = y
            outs[1][rows, :] = _rmsnorm(y, g_ref[0]).astype(BF16)


def _out_proj(merged, x, w_o, g, layer, g_layer, total_rows, final_norm):
    rows = merged.shape[0]
    tm = TM_OUT
    out_block = pl.BlockSpec((tm, D_MODEL), lambda i: (i, 0))
    if final_norm:
        out_specs = out_block
        out_shape = jax.ShapeDtypeStruct((rows, D_MODEL), F32)
    else:
        out_specs = [out_block, out_block]
        out_shape = [jax.ShapeDtypeStruct((total_rows, D_MODEL), F32),
                     jax.ShapeDtypeStruct((total_rows, D_MODEL), BF16)]
    return pl.pallas_call(
        functools.partial(_out_proj_kernel, final_norm=final_norm),
        grid=(rows // tm,),
        in_specs=[
            pl.BlockSpec((tm, D_MODEL), lambda i: (i, 0)),
            pl.BlockSpec((tm, D_MODEL), lambda i: (i, 0)),
            _resident((None, D_MODEL, D_MODEL), lambda i: (layer, 0, 0)),
            pl.BlockSpec((1, 1, D_MODEL), lambda i: (g_layer, 0, 0)),
        ],
        out_specs=out_specs,
        out_shape=out_shape,
        compiler_params=_params("parallel"),
        name="out_proj",
    )(merged, x, w_o, g)


def _retention_kernel(q_ref, k_ref, v_ref, gr_ref, mask_ref, qdec_ref, kdec_ref,
                      sdec_ref, s0_ref, o_ref, s_ref):
    @pl.when(pl.program_id(1) == 0)
    def _():
        s_ref[...] = s0_ref[...]

    _retention_tile(q_ref, k_ref, v_ref, gr_ref, mask_ref, qdec_ref, kdec_ref,
                    sdec_ref, s_ref, o_ref)


def _retention(qk, v, gates, s0, tables, row0, batch, seq, ts):
    mask, qdec, kdec, sdec = tables
    ns = seq // ts
    r0 = row0 // ts
    row = lambda b, s: r0 + b * ns + s
    const3 = lambda b, s: (0, 0, 0)
    return pl.pallas_call(
        _retention_kernel,
        grid=(batch, ns),
        in_specs=[
            pl.BlockSpec((ts, QKW), lambda b, s: (row(b, s), 0)),
            pl.BlockSpec((ts, QKW), lambda b, s: (row(b, s), 1)),
            pl.BlockSpec((ts, RV), lambda b, s: (row(b, s), 0)),
            pl.BlockSpec((ts, RV), lambda b, s: (row(b, s), 0)),
            pl.BlockSpec((H_R, ts, ts), const3),
            pl.BlockSpec((H_R, ts, DK), const3),
            pl.BlockSpec((H_R, ts, DK), const3),
            pl.BlockSpec((H_R, 1, DV), const3),
            pl.BlockSpec((1, H_R, DK, DV), lambda b, s: (b, 0, 0, 0)),
        ],
        out_specs=[
            pl.BlockSpec((ts, RV), lambda b, s: (b * ns + s, 0)),
            pl.BlockSpec((1, H_R, DK, DV), lambda b, s: (b, 0, 0, 0)),
        ],
        out_shape=[
            jax.ShapeDtypeStruct((batch * seq, RV), BF16),
            jax.ShapeDtypeStruct((batch, H_R, DK, DV), F32),
        ],
        compiler_params=_params("parallel", "arbitrary"),
        name="retention",
    )(qk, qk, v, gates, mask, qdec, kdec, sdec, s0)


def _gmlp_kernel(u_ref, vm_ref, gm_ref, ws_ref, bias_ref, lng_ref, lnb_ref,
                 m_ref, vn_ref, w2_scr, *, ts, lg):
    @pl.when((pl.program_id(0) == 0) & (pl.program_id(1) == 0))
    def _():
        _build_mixing_matrix(ws_ref, w2_scr, ts, lg)

    _gmlp_tile(u_ref, vm_ref, gm_ref, bias_ref, lng_ref, lnb_ref, w2_scr, m_ref,
               vn_ref)


def _gmlp(uv, gates, ws, ws_b, ln_g, ln_b, layer, row0, batch, seq, ts):
    lg = min(GMLP_CHUNK, seq)
    ns = seq // ts
    t = batch * seq
    r0 = row0 // ts
    row = lambda b, s: r0 + b * ns + s
    out_block = pl.BlockSpec((ts, D_MLP), lambda b, s: (b * ns + s, 0))
    return pl.pallas_call(
        functools.partial(_gmlp_kernel, ts=ts, lg=lg),
        grid=(batch, ns),
        in_specs=[
            pl.BlockSpec((ts, D_MLP), lambda b, s: (row(b, s), 0)),
            pl.BlockSpec((ts, D_MLP), lambda b, s: (row(b, s), 1)),
            pl.BlockSpec((ts, D_MLP), lambda b, s: (row(b, s), 1)),
            pl.BlockSpec((None, G_MLP, GMLP_CHUNK, GMLP_CHUNK),
                         lambda b, s: (layer, 0, 0, 0)),
            pl.BlockSpec((ts, G_MLP), lambda b, s: (0, 0)),
            pl.BlockSpec((1, 1, D_MLP), lambda b, s: (layer, 0, 0)),
            pl.BlockSpec((1, 1, D_MLP), lambda b, s: (layer, 0, 0)),
        ],
        out_specs=[out_block, out_block],
        out_shape=[jax.ShapeDtypeStruct((t, D_MLP), BF16),
                   jax.ShapeDtypeStruct((t, D_MLP), F32)],
        scratch_shapes=[pltpu.VMEM((G_MLP, ts, ts), BF16)],
        compiler_params=_params("arbitrary", "arbitrary"),
        name="gmlp",
    )(uv, uv, gates, ws, _mixing_bias(ws_b, layer, ts, lg), ln_g, ln_b)


def _merge_out_kernel(r_ref, m_ref, ar_ref, am_ref, x_ref, wr_ref, wm_ref,
                      wo_ref, g_ref, *rest, final_norm):
    outs = rest[-1:] if final_norm else rest[-2:]
    rb = jnp.dot(r_ref[...], wr_ref[...], preferred_element_type=F32)
    mb = jnp.dot(m_ref[...], wm_ref[...], preferred_element_type=F32)
    merged = ar_ref[...].astype(F32) * rb + am_ref[...].astype(F32) * mb
    y = x_ref[...] + jnp.dot(merged.astype(BF16), wo_ref[...],
                             preferred_element_type=F32)
    if final_norm:
        outs[0][...] = _rmsnorm(y, g_ref[0])
    else:
        outs[0][...] = y
        outs[1][...] = _rmsnorm(y, g_ref[0]).astype(BF16)


def _merge_out(r, m, a_sig, x, x_row0, w_ret, w_mlp, w_o, g, layer, g_layer,
               row0, dest, final_norm):
    tm = r.shape[0]
    r0 = row0 // tm
    xr0 = x_row0 // tm
    weight = lambda: _resident((None, D_MODEL, D_MODEL), lambda i: (layer, 0, 0))
    in_specs = [
        pl.BlockSpec((tm, RV), lambda i: (i, 0)),
        pl.BlockSpec((tm, D_MLP), lambda i: (i, 0)),
        pl.BlockSpec((tm, D_MODEL), lambda i: (r0 + i, 0)),
        pl.BlockSpec((tm, D_MODEL), lambda i: (r0 + i, 1)),
        pl.BlockSpec((tm, D_MODEL), lambda i: (xr0 + i, 0)),
        weight(), weight(), weight(),
        pl.BlockSpec((1, 1, D_MODEL), lambda i: (g_layer, 0, 0)),
    ]
    args = [r, m, a_sig, a_sig, x, w_ret, w_mlp, w_o, g]
    if final_norm:
        out_specs = pl.BlockSpec((tm, D_MODEL), lambda i: (i, 0))
        out_shape = jax.ShapeDtypeStruct((tm, D_MODEL), F32)
        aliases = {}
    else:
        out_block = pl.BlockSpec((tm, D_MODEL), lambda i: (r0 + i, 0))
        out_specs = [out_block, out_block]
        out_shape = [jax.ShapeDtypeStruct(dest[0].shape, F32),
                     jax.ShapeDtypeStruct(dest[1].shape, BF16)]
        in_specs += [_ANY, _ANY]
        args += list(dest)
        aliases = {len(args) - 2: 0, len(args) - 1: 1}
    return pl.pallas_call(
        functools.partial(_merge_out_kernel, final_norm=final_norm),
        grid=(1,),
        in_specs=in_specs,
        out_specs=out_specs,
        out_shape=out_shape,
        input_output_aliases=aliases,
        compiler_params=_params("arbitrary"),
        name="merge_out",
    )(*args)


def kernel(x_prompt, x_sample, state_ret, norm_g, w_in, ws, ws_b, ln_g, ln_b,
           w_ret_out, w_mlp_out, w_o, final_g):
    b_p, s_p, _ = x_prompt.shape
    b_s, s_s, _ = x_sample.shape
    t_p = b_p * s_p
    t_s = b_s * s_s
    t_all = t_p + t_s
    tab_p = _retention_tables(TS_MIX, CHUNK)
    tab_s = _retention_tables(s_s, s_s)
    cos_p, sin_p = _rotary_tables(s_p, 0)
    cos_s, sin_s = _rotary_tables(s_s, PAST_LEN)
    cos_rows = jnp.concatenate([jnp.tile(cos_p, (b_p, 1)), jnp.tile(cos_s, (b_s, 1))])
    sin_rows = jnp.concatenate([jnp.tile(sin_p, (b_p, 1)), jnp.tile(sin_s, (b_s, 1))])
    norm_g3 = norm_g.reshape(DEPTH, 1, D_MODEL)
    ln_g3 = ln_g.reshape(DEPTH, 1, D_MLP)
    ln_b3 = ln_b.reshape(DEPTH, 1, D_MLP)
    fg3 = final_g.reshape(1, 1, D_MODEL)
    w_ret_b = w_ret_out.astype(BF16)
    w_mlp_b = w_mlp_out.astype(BF16)
    w_o_b = w_o.astype(BF16)

    x_p, x_s, xs_row0 = x_prompt.reshape(t_p, D_MODEL), x_sample.reshape(t_s, D_MODEL), 0
    h = _prenorm(x_p, norm_g3, 0, TM_OUT, t_all, 0)
    h = _prenorm(x_s, norm_g3, 0, t_s, t_all, t_p, dest=h)
    st_p, st_s, v_s = [], [], []
    for l in range(DEPTH):
        last = l == DEPTH - 1
        qk = _in_proj_qk(h, w_in, l, cos_rows, sin_rows)
        v = _in_proj_act(h, w_in, l, (OFF_V,), RV, "copy")
        gates = _in_proj_act(h, w_in, l, (OFF_GR, OFF_GM), RV, "silu")
        uv = _in_proj_act(h, w_in, l, (OFF_U, OFF_VM), D_MLP, "gelu")
        a_sig = _in_proj_act(h, w_in, l, (OFF_AR, OFF_AM), D_MODEL, "sigmoid")

        merged_p, sp = _mixer(qk, v, gates, uv, a_sig, tab_p, ws, ws_b, ln_g3, ln_b3,
                              w_ret_b, w_mlp_b, l, b_p, s_p)
        r_s, ss = _retention(qk, v, gates, state_ret[l], tab_s, t_p, b_s, s_s, s_s)
        m_s, vs = _gmlp(uv, gates, ws, ws_b, ln_g3, ln_b3, l, t_p, b_s, s_s, s_s)
        st_p.append(sp)
        st_s.append(ss)
        v_s.append(vs.reshape(b_s, s_s, D_MLP))

        g, g_layer = (fg3, 0) if last else (norm_g3, l + 1)
        out_p = _out_proj(merged_p, x_p, w_o_b, g, l, g_layer, t_all, last)
        out_s = _merge_out(r_s, m_s, a_sig, x_s, xs_row0, w_ret_b, w_mlp_b, w_o_b,
                           g, l, g_layer, t_p, None if last else out_p, last)
        if last:
            y_p, y_s = out_p, out_s
        else:
            x_all, h = out_s
            x_p, x_s, xs_row0 = x_all, x_all, t_p
    return (y_p.reshape(b_p, s_p, D_MODEL), y_s.reshape(b_s, s_s, D_MODEL),
            jnp.stack(st_p), jnp.stack(st_s), jnp.stack(v_s))
```

```python
import functools

import jax
import jax.numpy as jnp
import numpy as np
from jax import lax
from jax.experimental import pallas as pl
from jax.experimental.pallas import tpu as pltpu

D_MODEL = 2048
DEPTH = 4
PAST_LEN = 1024
CHUNK = 64
H_R = 8
DK = D_MODEL // 16
DV = D_MODEL // 8
QKW = H_R * DK
RV = H_R * DV
D_MLP = D_MODEL
G_MLP = 8
DG = D_MLP // G_MLP
GMLP_CHUNK = 128
ROPE_BASE = 10000.0
EPS = 1e-6
N_IN = QKW * 2 + RV * 2 + D_MLP * 3 + D_MODEL * 2

OFF_Q = 0
OFF_K = OFF_Q + QKW
OFF_V = OFF_K + QKW
OFF_GR = OFF_V + RV
OFF_U = OFF_GR + RV
OFF_VM = OFF_U + D_MLP
OFF_GM = OFF_VM + D_MLP
OFF_AR = OFF_GM + D_MLP
OFF_AM = OFF_AR + D_MODEL

VMEM_LIMIT = 56 * 1024 * 1024

TM_IN = 2080
RC_IN = 208
TN_IN = 1024
RC = 256
TS_MIX = 256
TM_OUT = 512

F32 = jnp.float32
BF16 = jnp.bfloat16


def _rmsnorm(x, g):
    return x * lax.rsqrt(jnp.mean(x * x, axis=-1, keepdims=True) + EPS) * g


def _params(*sem):
    return pltpu.CompilerParams(dimension_semantics=sem,
                                vmem_limit_bytes=VMEM_LIMIT)


def _resident(block_shape, index_map):
    return pl.BlockSpec(block_shape, index_map, pipeline_mode=pl.Buffered(1))


_ANY = pl.BlockSpec(memory_space=pl.ANY)


def _prenorm_kernel(x_ref, g_ref, *rest):
    h_ref = rest[-1]
    h_ref[...] = _rmsnorm(x_ref[...], g_ref[0]).astype(BF16)


def _prenorm(x, norm_g, layer, tm, total_rows, row0, dest=None):
    r0 = row0 // tm
    in_specs = [
        pl.BlockSpec((tm, D_MODEL), lambda i: (i, 0)),
        pl.BlockSpec((1, 1, D_MODEL), lambda i: (layer, 0, 0)),
    ]
    args = [x, norm_g]
    aliases = {}
    if dest is not None:
        in_specs.append(_ANY)
        args.append(dest)
        aliases = {2: 0}
    return pl.pallas_call(
        _prenorm_kernel,
        grid=(x.shape[0] // tm,),
        in_specs=in_specs,
        out_specs=pl.BlockSpec((tm, D_MODEL), lambda i: (r0 + i, 0)),
        out_shape=jax.ShapeDtypeStruct((total_rows, D_MODEL), BF16),
        input_output_aliases=aliases,
        compiler_params=_params("parallel"),
        name="prenorm",
    )(*args)


def _cast_weight_tile(w_ref, wb_scr):
    @pl.when(pl.program_id(1) == 0)
    def _():
        wb_scr[...] = w_ref[...].astype(BF16)


def _in_proj_qk_kernel(h_ref, w_ref, cos_ref, sin_ref, z_ref, wb_scr):
    _cast_weight_tile(w_ref, wb_scr)
    scale = jnp.where(pl.program_id(0) == 1, np.float32(DK ** -0.5), np.float32(1.0))
    for c in range(TM_IN // RC_IN):
        rows = slice(c * RC_IN, (c + 1) * RC_IN)
        x = jnp.dot(h_ref[rows, :], wb_scr[...], preferred_element_type=F32)
        cos = cos_ref[rows, :]
        sin = sin_ref[rows, :]
        for h in range(TN_IN // DK):
            ks = slice(h * DK, (h + 1) * DK)
            xs = x[:, ks]
            r = xs * cos + pltpu.roll(xs, DK // 2, 1) * sin
            z_ref[rows, ks] = (r * scale).astype(BF16)


def _activation(act, x):
    if act == "copy":
        return x
    if act == "sigmoid":
        return jax.nn.sigmoid(x)
    if act == "silu":
        return x * jax.nn.sigmoid(x)
    if act == "gelu":
        return 0.5 * x * (1.0 + lax.erf(x * np.float32(np.sqrt(0.5))))
    raise ValueError(act)


def _in_proj_act_kernel(h_ref, w_ref, z_ref, wb_scr, *, act):
    _cast_weight_tile(w_ref, wb_scr)
    for c in range(TM_IN // RC_IN):
        rows = slice(c * RC_IN, (c + 1) * RC_IN)
        x = jnp.dot(h_ref[rows, :], wb_scr[...], preferred_element_type=F32)
        z_ref[rows, :] = _activation(act, x).astype(BF16)


def _in_proj_qk(h, w_in, layer, cos_rows, sin_rows):
    t = h.shape[0]
    return pl.pallas_call(
        _in_proj_qk_kernel,
        grid=(2 * QKW // TN_IN, t // TM_IN),
        in_specs=[
            pl.BlockSpec((TM_IN, D_MODEL), lambda j, i: (i, 0)),
            pl.BlockSpec((None, D_MODEL, TN_IN), lambda j, i: (layer, 0, j)),
            pl.BlockSpec((TM_IN, DK), lambda j, i: (i, 0)),
            pl.BlockSpec((TM_IN, DK), lambda j, i: (i, 0)),
        ],
        out_specs=pl.BlockSpec((TM_IN, TN_IN), lambda j, i: (i, j)),
        out_shape=jax.ShapeDtypeStruct((t, 2 * QKW), BF16),
        scratch_shapes=[pltpu.VMEM((D_MODEL, TN_IN), BF16)],
        compiler_params=_params("arbitrary", "arbitrary"),
        name="in_proj_qk",
    )(h, w_in, cos_rows, sin_rows)


def _in_proj_act(h, w_in, layer, offsets, width, act):
    t = h.shape[0]
    per = width // TN_IN

    def src_tile(j):
        tile = offsets[0] // TN_IN
        for n, off in enumerate(offsets[1:], start=1):
            tile = jnp.where(j // per == n, off // TN_IN, tile)
        return tile + j % per

    return pl.pallas_call(
        functools.partial(_in_proj_act_kernel, act=act),
        grid=(len(offsets) * per, t // TM_IN),
        in_specs=[
            pl.BlockSpec((TM_IN, D_MODEL), lambda j, i: (i, 0)),
            pl.BlockSpec((None, D_MODEL, TN_IN), lambda j, i: (layer, 0, src_tile(j))),
        ],
        out_specs=pl.BlockSpec((TM_IN, TN_IN), lambda j, i: (i, j)),
        out_shape=jax.ShapeDtypeStruct((t, len(offsets) * width), BF16),
        scratch_shapes=[pltpu.VMEM((D_MODEL, TN_IN), BF16)],
        compiler_params=_params("arbitrary", "arbitrary"),
        name="in_proj_" + act,
    )(h, w_in)


def _retention_tile(q_ref, k_ref, v_ref, gr_ref, mask_ref, qdec_ref, kdec_ref,
                    sdec_ref, s_ref, o_ref):
    for h in range(H_R):
        ks = slice(h * DK, (h + 1) * DK)
        vs = slice(h * DV, (h + 1) * DV)
        qb = q_ref[:, ks]
        kb = k_ref[:, ks]
        vb = v_ref[:, vs]
        state = s_ref[0, h]

        scores = lax.dot_general(qb, kb, (((1,), (1,)), ((), ())),
                                 preferred_element_type=F32)
        p = (scores * mask_ref[h]).astype(BF16)
        intra = jnp.dot(p, vb, preferred_element_type=F32)
        qd = (qb.astype(F32) * qdec_ref[h]).astype(BF16)
        cross = jnp.dot(qd, state.astype(BF16), preferred_element_type=F32)
        o = intra + cross

        kd = (kb.astype(F32) * kdec_ref[h]).astype(BF16)
        s_ref[0, h] = sdec_ref[h] * state + lax.dot_general(
            kd, vb, (((0,), (0,)), ((), ())), preferred_element_type=F32)

        mu = jnp.mean(o, axis=-1, keepdims=True)
        d = o - mu
        var = jnp.mean(d * d, axis=-1, keepdims=True)
        on = d * lax.rsqrt(var + EPS)
        o_ref[:, vs] = (gr_ref[:, vs].astype(F32) * on).astype(BF16)


def _build_mixing_matrix(ws_ref, w2_scr, ts, lg):
    i = lax.broadcasted_iota(jnp.int32, (lg, lg), 0)
    j = lax.broadcasted_iota(jnp.int32, (lg, lg), 1)
    keep = (i // CHUNK) >= (j // CHUNK)
    w2_scr[...] = jnp.zeros_like(w2_scr)
    for g in range(G_MLP):
        wg = jnp.where(keep, ws_ref[g, :lg, :lg], 0.0).astype(BF16)
        for c in range(ts // lg):
            w2_scr[g, c * lg:(c + 1) * lg, c * lg:(c + 1) * lg] = wg


def _gmlp_tile(u_ref, vm_ref, gm_ref, bias_ref, lng_ref, lnb_ref, w2_scr, m_ref,
               vn_ref=None):
    gv = vm_ref[...].astype(F32)
    mu = jnp.mean(gv, axis=-1, keepdims=True)
    d = gv - mu
    var = jnp.mean(d * d, axis=-1, keepdims=True)
    vn = d * lax.rsqrt(var + EPS) * lng_ref[0] + lnb_ref[0]
    if vn_ref is not None:
        vn_ref[...] = vn
    for g in range(G_MLP):
        sl = slice(g * DG, (g + 1) * DG)
        s = jnp.dot(w2_scr[g], vn[:, sl].astype(BF16),
                    preferred_element_type=F32) + bias_ref[:, g:g + 1]
        m = u_ref[:, sl].astype(F32) * s
        m_ref[:, sl] = (gm_ref[:, sl].astype(F32) * m).astype(BF16)


def _retention_tables(ts, chunk):
    log_gamma = jnp.log(1.0 - 2.0 ** (-5.0 - jnp.arange(H_R, dtype=F32)))
    idx = jnp.arange(ts, dtype=F32)
    diff = idx[:, None] - idx[None, :]
    cn = (jnp.arange(ts) // chunk)[:, None]
    cm = (jnp.arange(ts) // chunk)[None, :]
    expo = jnp.where(cn == cm, jnp.abs(diff), diff)
    mask = jnp.where((cn >= cm)[None],
                     jnp.exp(log_gamma[:, None, None] * expo[None]), 0.0)
    ones = jnp.ones((1, 1, DK), F32)
    qdec = jnp.exp(log_gamma[:, None] * (idx[None, :] + 1.0))[..., None] * ones
    kdec = jnp.exp(log_gamma[:, None] * (ts - 1.0 - idx[None, :]))[..., None] * ones
    sdec = jnp.exp(log_gamma * ts)[:, None, None] * jnp.ones((1, 1, DV), F32)
    return mask, qdec, kdec, sdec


def _rotary_tables(seq, pos0):
    half = DK // 2
    inv_freq = 1.0 / (ROPE_BASE ** (jnp.arange(half, dtype=F32) / half))
    pos = (pos0 + jnp.arange(seq)).astype(F32)
    ang = pos[:, None] * inv_freq[None, :]
    cos = jnp.cos(ang)
    sin = jnp.sin(ang)
    return (jnp.concatenate([cos, cos], axis=-1),
            jnp.concatenate([-sin, sin], axis=-1))


def _mixing_bias(ws_b, layer, ts, lg):
    return jnp.tile(jnp.transpose(ws_b[layer, :, :lg]), (ts // lg, 1))


def _mixer_kernel(q_ref, k_ref, v_ref, gr_ref, gm_ref, u_ref, vm_ref, ar_ref,
                  am_ref, mask_ref, qdec_ref, kdec_ref, sdec_ref, ws_ref,
                  bias_ref, lng_ref, lnb_ref, wr_ref, wm_ref,
                  merged_ref, s_ref, w2_scr, r_scr, m_scr, *, ts, lg):
    @pl.when((pl.program_id(0) == 0) & (pl.program_id(1) == 0))
    def _():
        _build_mixing_matrix(ws_ref, w2_scr, ts, lg)

    @pl.when(pl.program_id(1) == 0)
    def _():
        s_ref[...] = jnp.zeros_like(s_ref)

    _retention_tile(q_ref, k_ref, v_ref, gr_ref, mask_ref, qdec_ref, kdec_ref,
                    sdec_ref, s_ref, r_scr)
    rb = jnp.dot(r_scr[...], wr_ref[...], preferred_element_type=F32)
    _gmlp_tile(u_ref, vm_ref, gm_ref, bias_ref, lng_ref, lnb_ref, w2_scr, m_scr)
    mb = jnp.dot(m_scr[...], wm_ref[...], preferred_element_type=F32)
    merged = ar_ref[...].astype(F32) * rb + am_ref[...].astype(F32) * mb
    merged_ref[...] = merged.astype(BF16)


def _mixer(qk, v, gates, uv, a_sig, tables, ws, ws_b, ln_g, ln_b, w_ret, w_mlp,
           layer, batch, seq):
    mask, qdec, kdec, sdec = tables
    ts = TS_MIX
    lg = GMLP_CHUNK
    ns = seq // ts
    row = lambda b, s: b * ns + s
    const3 = lambda b, s: (0, 0, 0)
    bias = _mixing_bias(ws_b, layer, ts, lg)
    tile = lambda width, col: pl.BlockSpec((ts, width), lambda b, s: (row(b, s), col))
    return pl.pallas_call(
        functools.partial(_mixer_kernel, ts=ts, lg=lg),
        grid=(batch, ns),
        in_specs=[
            tile(QKW, 0), tile(QKW, 1),
            tile(RV, 0),
            tile(RV, 0), tile(D_MLP, 1),
            tile(D_MLP, 0), tile(D_MLP, 1),
            tile(D_MODEL, 0), tile(D_MODEL, 1),
            _resident((H_R, ts, ts), const3),
            _resident((H_R, ts, DK), const3),
            _resident((H_R, ts, DK), const3),
            _resident((H_R, 1, DV), const3),
            _resident((None, G_MLP, GMLP_CHUNK, GMLP_CHUNK),
                      lambda b, s: (layer, 0, 0, 0)),
            _resident((ts, G_MLP), lambda b, s: (0, 0)),
            _resident((1, 1, D_MLP), lambda b, s: (layer, 0, 0)),
            _resident((1, 1, D_MLP), lambda b, s: (layer, 0, 0)),
            _resident((None, RV, D_MODEL), lambda b, s: (layer, 0, 0)),
            _resident((None, D_MLP, D_MODEL), lambda b, s: (layer, 0, 0)),
        ],
        out_specs=[
            pl.BlockSpec((ts, D_MODEL), lambda b, s: (row(b, s), 0)),
            pl.BlockSpec((1, H_R, DK, DV), lambda b, s: (b, 0, 0, 0)),
        ],
        out_shape=[
            jax.ShapeDtypeStruct((batch * seq, D_MODEL), BF16),
            jax.ShapeDtypeStruct((batch, H_R, DK, DV), F32),
        ],
        scratch_shapes=[pltpu.VMEM((G_MLP, ts, ts), BF16),
                        pltpu.VMEM((ts, RV), BF16),
                        pltpu.VMEM((ts, D_MLP), BF16)],
        compiler_params=_params("arbitrary", "arbitrary"),
        name="mixer",
    )(qk, qk, v, gates, gates, uv, uv, a_sig, a_sig, mask, qdec, kdec, sdec,
      ws, bias, ln_g, ln_b, w_ret, w_mlp)


def _out_proj_kernel(mg_ref, x_ref, wo_ref, g_ref, *outs, final_norm):
    for c in range(TM_OUT // RC):
        rows = slice(c * RC, (c + 1) * RC)
        y = x_ref[rows, :] + jnp.dot(mg_ref[rows, :], wo_ref[...],
                                     preferred_element_type=F32)
        if final_norm:
            outs[0][rows, :] = _rmsnorm(y, g_ref[0])
        else:
            outs[0][rows, :] = y
            outs[1][rows, :] = _rmsnorm(y, g_ref[0]).astype(BF16)


def _out_proj(merged, x, w_o, g, layer, g_layer, total_rows, final_norm):
    rows = merged.shape[0]
    tm = TM_OUT
    out_block = pl.BlockSpec((tm, D_MODEL), lambda i: (i, 0))
    if final_norm:
        out_specs = out_block
        out_shape = jax.ShapeDtypeStruct((rows, D_MODEL), F32)
    else:
        out_specs = [out_block, out_block]
        out_shape = [jax.ShapeDtypeStruct((total_rows, D_MODEL), F32),
                     jax.ShapeDtypeStruct((total_rows, D_MODEL), BF16)]
    return pl.pallas_call(
        functools.partial(_out_proj_kernel, final_norm=final_norm),
        grid=(rows // tm,),
        in_specs=[
            pl.BlockSpec((tm, D_MODEL), lambda i: (i, 0)),
            pl.BlockSpec((tm, D_MODEL), lambda i: (i, 0)),
            _resident((None, D_MODEL, D_MODEL), lambda i: (layer, 0, 0)),
            pl.BlockSpec((1, 1, D_MODEL), lambda i: (g_layer, 0, 0)),
        ],
        out_specs=out_specs,
        out_shape=out_shape,
        compiler_params=_params("parallel"),
        name="out_proj",
    )(merged, x, w_o, g)


def _retention_kernel(q_ref, k_ref, v_ref, gr_ref, mask_ref, qdec_ref, kdec_ref,
                      sdec_ref, s0_ref, o_ref, s_ref):
    @pl.when(pl.program_id(1) == 0)
    def _():
        s_ref[...] = s0_ref[...]

    _retention_tile(q_ref, k_ref, v_ref, gr_ref, mask_ref, qdec_ref, kdec_ref,
                    sdec_ref, s_ref, o_ref)


def _retention(qk, v, gates, s0, tables, row0, batch, seq, ts):
    mask, qdec, kdec, sdec = tables
    ns = seq // ts
    r0 = row0 // ts
    row = lambda b, s: r0 + b * ns + s
    const3 = lambda b, s: (0, 0, 0)
    return pl.pallas_call(
        _retention_kernel,
        grid=(batch, ns),
        in_specs=[
            pl.BlockSpec((ts, QKW), lambda b, s: (row(b, s), 0)),
            pl.BlockSpec((ts, QKW), lambda b, s: (row(b, s), 1)),
            pl.BlockSpec((ts, RV), lambda b, s: (row(b, s), 0)),
            pl.BlockSpec((ts, RV), lambda b, s: (row(b, s), 0)),
            pl.BlockSpec((H_R, ts, ts), const3),
            pl.BlockSpec((H_R, ts, DK), const3),
            pl.BlockSpec((H_R, ts, DK), const3),
            pl.BlockSpec((H_R, 1, DV), const3),
            pl.BlockSpec((1, H_R, DK, DV), lambda b, s: (b, 0, 0, 0)),
        ],
        out_specs=[
            pl.BlockSpec((ts, RV), lambda b, s: (b * ns + s, 0)),
            pl.BlockSpec((1, H_R, DK, DV), lambda b, s: (b, 0, 0, 0)),
        ],
        out_shape=[
            jax.ShapeDtypeStruct((batch * seq, RV), BF16),
            jax.ShapeDtypeStruct((batch, H_R, DK, DV), F32),
        ],
        compiler_params=_params("parallel", "arbitrary"),
        name="retention",
    )(qk, qk, v, gates, mask, qdec, kdec, sdec, s0)


def _gmlp_kernel(u_ref, vm_ref, gm_ref, ws_ref, bias_ref, lng_ref, lnb_ref,
                 m_ref, vn_ref, w2_scr, *, ts, lg):
    @pl.when((pl.program_id(0) == 0) & (pl.program_id(1) == 0))
    def _():
        _build_mixing_matrix(ws_ref, w2_scr, ts, lg)

    _gmlp_tile(u_ref, vm_ref, gm_ref, bias_ref, lng_ref, lnb_ref, w2_scr, m_ref,
               vn_ref)


def _gmlp(uv, gates, ws, ws_b, ln_g, ln_b, layer, row0, batch, seq, ts):
    lg = min(GMLP_CHUNK, seq)
    ns = seq // ts
    t = batch * seq
    r0 = row0 // ts
    row = lambda b, s: r0 + b * ns + s
    out_block = pl.BlockSpec((ts, D_MLP), lambda b, s: (b * ns + s, 0))
    return pl.pallas_call(
        functools.partial(_gmlp_kernel, ts=ts, lg=lg),
        grid=(batch, ns),
        in_specs=[
            pl.BlockSpec((ts, D_MLP), lambda b, s: (row(b, s), 0)),
            pl.BlockSpec((ts, D_MLP), lambda b, s: (row(b, s), 1)),
            pl.BlockSpec((ts, D_MLP), lambda b, s: (row(b, s), 1)),
            pl.BlockSpec((None, G_MLP, GMLP_CHUNK, GMLP_CHUNK),
                         lambda b, s: (layer, 0, 0, 0)),
            pl.BlockSpec((ts, G_MLP), lambda b, s: (0, 0)),
            pl.BlockSpec((1, 1, D_MLP), lambda b, s: (layer, 0, 0)),
            pl.BlockSpec((1, 1, D_MLP), lambda b, s: (layer, 0, 0)),
        ],
        out_specs=[out_block, out_block],
        out_shape=[jax.ShapeDtypeStruct((t, D_MLP), BF16),
                   jax.ShapeDtypeStruct((t, D_MLP), F32)],
        scratch_shapes=[pltpu.VMEM((G_MLP, ts, ts), BF16)],
        compiler_params=_params("arbitrary", "arbitrary"),
        name="gmlp",
    )(uv, uv, gates, ws, _mixing_bias(ws_b, layer, ts, lg), ln_g, ln_b)


def _merge_out_kernel(r_ref, m_ref, ar_ref, am_ref, x_ref, wr_ref, wm_ref,
                      wo_ref, g_ref, *rest, final_norm):
    outs = rest[-1:] if final_norm else rest[-2:]
    rb = jnp.dot(r_ref[...], wr_ref[...], preferred_element_type=F32)
    mb = jnp.dot(m_ref[...], wm_ref[...], preferred_element_type=F32)
    merged = ar_ref[...].astype(F32) * rb + am_ref[...].astype(F32) * mb
    y = x_ref[...] + jnp.dot(merged.astype(BF16), wo_ref[...],
                             preferred_element_type=F32)
    if final_norm:
        outs[0][...] = _rmsnorm(y, g_ref[0])
    else:
        outs[0][...] = y
        outs[1][...] = _rmsnorm(y, g_ref[0]).astype(BF16)


def _merge_out(r, m, a_sig, x, x_row0, w_ret, w_mlp, w_o, g, layer, g_layer,
               row0, dest, final_norm):
    tm = r.shape[0]
    r0 = row0 // tm
    xr0 = x_row0 // tm
    weight = lambda: _resident((None, D_MODEL, D_MODEL), lambda i: (layer, 0, 0))
    in_specs = [
        pl.BlockSpec((tm, RV), lambda i: (i, 0)),
        pl.BlockSpec((tm, D_MLP), lambda i: (i, 0)),
        pl.BlockSpec((tm, D_MODEL), lambda i: (r0 + i, 0)),
        pl.BlockSpec((tm, D_MODEL), lambda i: (r0 + i, 1)),
        pl.BlockSpec((tm, D_MODEL), lambda i: (xr0 + i, 0)),
        weight(), weight(), weight(),
        pl.BlockSpec((1, 1, D_MODEL), lambda i: (g_layer, 0, 0)),
    ]
    args = [r, m, a_sig, a_sig, x, w_ret, w_mlp, w_o, g]
    if final_norm:
        out_specs = pl.BlockSpec((tm, D_MODEL), lambda i: (i, 0))
        out_shape = jax.ShapeDtypeStruct((tm, D_MODEL), F32)
        aliases = {}
    else:
        out_block = pl.BlockSpec((tm, D_MODEL), lambda i: (r0 + i, 0))
        out_specs = [out_block, out_block]
        out_shape = [jax.ShapeDtypeStruct(dest[0].shape, F32),
                     jax.ShapeDtypeStruct(dest[1].shape, BF16)]
        in_specs += [_ANY, _ANY]
        args += list(dest)
        aliases = {len(args) - 2: 0, len(args) - 1: 1}
    return pl.pallas_call(
        functools.partial(_merge_out_kernel, final_norm=final_norm),
        grid=(1,),
        in_specs=in_specs,
        out_specs=out_specs,
        out_shape=out_shape,
        input_output_aliases=aliases,
        compiler_params=_params("arbitrary"),
        name="merge_out",
    )(*args)


def kernel(x_prompt, x_sample, state_ret, norm_g, w_in, ws, ws_b, ln_g, ln_b,
           w_ret_out, w_mlp_out, w_o, final_g):
    b_p, s_p, _ = x_prompt.shape
    b_s, s_s, _ = x_sample.shape
    t_p = b_p * s_p
    t_s = b_s * s_s
    t_all = t_p + t_s
    tab_p = _retention_tables(TS_MIX, CHUNK)
    tab_s = _retention_tables(s_s, s_s)
    cos_p, sin_p = _rotary_tables(s_p, 0)
    cos_s, sin_s = _rotary_tables(s_s, PAST_LEN)
    cos_rows = jnp.concatenate([jnp.tile(cos_p, (b_p, 1)), jnp.tile(cos_s, (b_s, 1))])
    sin_rows = jnp.concatenate([jnp.tile(sin_p, (b_p, 1)), jnp.tile(sin_s, (b_s, 1))])
    norm_g3 = norm_g.reshape(DEPTH, 1, D_MODEL)
    ln_g3 = ln_g.reshape(DEPTH, 1, D_MLP)
    ln_b3 = ln_b.reshape(DEPTH, 1, D_MLP)
    fg3 = final_g.reshape(1, 1, D_MODEL)
    w_ret_b = w_ret_out.astype(BF16)
    w_mlp_b = w_mlp_out.astype(BF16)
    w_o_b = w_o.astype(BF16)

    x_p, x_s, xs_row0 = x_prompt.reshape(t_p, D_MODEL), x_sample.reshape(t_s, D_MODEL), 0
    h = _prenorm(x_p, norm_g3, 0, TM_OUT, t_all, 0)
    h = _prenorm(x_s, norm_g3, 0, t_s, t_all, t_p, dest=h)
    st_p, st_s, v_s = [], [], []
    for l in range(DEPTH):
        last = l == DEPTH - 1
        qk = _in_proj_qk(h, w_in, l, cos_rows, sin_rows)
        v = _in_proj_act(h, w_in, l, (OFF_V,), RV, "copy")
        gates = _in_proj_act(h, w_in, l, (OFF_GR, OFF_GM), RV, "silu")
        uv = _in_proj_act(h, w_in, l, (OFF_U, OFF_VM), D_MLP, "gelu")
        a_sig = _in_proj_act(h, w_in, l, (OFF_AR, OFF_AM), D_MODEL, "sigmoid")

        merged_p, sp = _mixer(qk, v, gates, uv, a_sig, tab_p, ws, ws_b, ln_g3, ln_b3,
                              w_ret_b, w_mlp_b, l, b_p, s_p)
        r_s, ss = _retention(qk, v, gates, state_ret[l], tab_s, t_p, b_s, s_s, s_s)
        m_s, vs = _gmlp(uv, gates, ws, ws_b, ln_g3, ln_b3, l, t_p, b_s, s_s, s_s)
        st_p.append(sp)
        st_s.append(ss)
        v_s.append(vs.reshape(b_s, s_s, D_MLP))

        g, g_layer = (fg3, 0) if last else (norm_g3, l + 1)
        out_p = _out_proj(merged_p, x_p, w_o_b, g, l, g_layer, t_all, last)
        out_s = _merge_out(r_s, m_s, a_sig, x_s, xs_row0, w_ret_b, w_mlp_b, w_o_b,
                           g, l, g_layer, t_p, None if last else out_p, last)
        if last:
            y_p, y_s = out_p, out_s
        else:
            x_all, h = out_s
            x_p, x_s, xs_row0 = x_all, x_all, t_p
    return (y_p.reshape(b_p, s_p, D_MODEL), y_s.reshape(b_s, s_s, D_MODEL),
            jnp.stack(st_p), jnp.stack(st_s), jnp.stack(v_s))
```

```python
import functools

import jax
import jax.numpy as jnp
import numpy as np
from jax import lax
from jax.experimental import pallas as pl
from jax.experimental.pallas import tpu as pltpu

D_MODEL = 2048
DEPTH = 4
PAST_LEN = 1024
CHUNK = 64
H_R = 8
DK = D_MODEL // 16
DV = D_MODEL // 8
QKW = H_R * DK
RV = H_R * DV
D_MLP = D_MODEL
G_MLP = 8
DG = D_MLP // G_MLP
GMLP_CHUNK = 128
ROPE_BASE = 10000.0
EPS = 1e-6
N_IN = QKW * 2 + RV * 2 + D_MLP * 3 + D_MODEL * 2

OFF_Q = 0
OFF_K = OFF_Q + QKW
OFF_V = OFF_K + QKW
OFF_GR = OFF_V + RV
OFF_U = OFF_GR + RV
OFF_VM = OFF_U + D_MLP
OFF_GM = OFF_VM + D_MLP
OFF_AR = OFF_GM + D_MLP
OFF_AM = OFF_AR + D_MODEL

VMEM_LIMIT = 56 * 1024 * 1024

TM_IN = 2080
RC_IN = 208
TN_IN = 1024
RC = 256
TS_MIX = 256
TM_OUT = 512

F32 = jnp.float32
BF16 = jnp.bfloat16


def _rmsnorm(x, g):
    return x * lax.rsqrt(jnp.mean(x * x, axis=-1, keepdims=True) + EPS) * g


def _params(*sem):
    return pltpu.CompilerParams(dimension_semantics=sem,
                                vmem_limit_bytes=VMEM_LIMIT)


def _resident(block_shape, index_map):
    return pl.BlockSpec(block_shape, index_map, pipeline_mode=pl.Buffered(1))


_ANY = pl.BlockSpec(memory_space=pl.ANY)


def _cast_weight_specs(weights, layer, n_steps, step_of):
    rows = D_MODEL // n_steps
    in_specs = [pl.BlockSpec((None, rows, D_MODEL),
                             lambda *ids: (layer, step_of(*ids), 0)) for _ in weights]
    out_specs = [pl.BlockSpec((rows, D_MODEL), lambda *ids: (step_of(*ids), 0))
                 for _ in weights]
    out_shape = [jax.ShapeDtypeStruct((D_MODEL, D_MODEL), BF16) for _ in weights]
    return in_specs, out_specs, out_shape


def _cast_weight_slabs(w_refs, wb_refs):
    for w_ref, wb_ref in zip(w_refs, wb_refs):
        wb_ref[...] = w_ref[...].astype(BF16)


def _prenorm_kernel(x_ref, g_ref, *rest, n_cast):
    h_ref = rest[len(rest) - n_cast - 1]
    h_ref[...] = _rmsnorm(x_ref[...], g_ref[0]).astype(BF16)
    _cast_weight_slabs(rest[:n_cast], rest[len(rest) - n_cast:])


def _prenorm(x, norm_g, layer, tm, total_rows, row0, dest=None, cast_weights=()):
    r0 = row0 // tm
    n_steps = x.shape[0] // tm
    w_in_specs, w_out_specs, w_out_shape = _cast_weight_specs(
        cast_weights, layer, n_steps, lambda i: i)
    in_specs = [
        pl.BlockSpec((tm, D_MODEL), lambda i: (i, 0)),
        pl.BlockSpec((1, 1, D_MODEL), lambda i: (layer, 0, 0)),
    ] + w_in_specs
    args = [x, norm_g] + list(cast_weights)
    aliases = {}
    if dest is not None:
        in_specs.append(_ANY)
        args.append(dest)
        aliases = {len(args) - 1: 0}
    outs = pl.pallas_call(
        functools.partial(_prenorm_kernel, n_cast=len(cast_weights)),
        grid=(n_steps,),
        in_specs=in_specs,
        out_specs=[pl.BlockSpec((tm, D_MODEL), lambda i: (r0 + i, 0))] + w_out_specs,
        out_shape=[jax.ShapeDtypeStruct((total_rows, D_MODEL), BF16)] + w_out_shape,
        input_output_aliases=aliases,
        compiler_params=_params("parallel"),
        name="prenorm",
    )(*args)
    return outs[0], tuple(outs[1:])


def _cast_weight_tile(w_ref, wb_scr):
    @pl.when(pl.program_id(1) == 0)
    def _():
        wb_scr[...] = w_ref[...].astype(BF16)


def _in_proj_qk_kernel(h_ref, w_ref, cos_ref, sin_ref, z_ref, wb_scr):
    _cast_weight_tile(w_ref, wb_scr)
    scale = jnp.where(pl.program_id(0) == 1, np.float32(DK ** -0.5), np.float32(1.0))
    for c in range(TM_IN // RC_IN):
        rows = slice(c * RC_IN, (c + 1) * RC_IN)
        x = jnp.dot(h_ref[rows, :], wb_scr[...], preferred_element_type=F32)
        cos = cos_ref[rows, :]
        sin = sin_ref[rows, :]
        for h in range(TN_IN // DK):
            ks = slice(h * DK, (h + 1) * DK)
            xs = x[:, ks]
            r = xs * cos + pltpu.roll(xs, DK // 2, 1) * sin
            z_ref[rows, ks] = (r * scale).astype(BF16)


def _activation(act, x):
    if act == "copy":
        return x
    if act == "sigmoid":
        return jax.nn.sigmoid(x)
    if act == "silu":
        return x * jax.nn.sigmoid(x)
    if act == "gelu":
        return 0.5 * x * (1.0 + lax.erf(x * np.float32(np.sqrt(0.5))))
    raise ValueError(act)


def _in_proj_act_kernel(h_ref, w_ref, z_ref, wb_scr, *, act):
    _cast_weight_tile(w_ref, wb_scr)
    for c in range(TM_IN // RC_IN):
        rows = slice(c * RC_IN, (c + 1) * RC_IN)
        x = jnp.dot(h_ref[rows, :], wb_scr[...], preferred_element_type=F32)
        z_ref[rows, :] = _activation(act, x).astype(BF16)


def _in_proj_qk(h, w_in, layer, cos_rows, sin_rows):
    t = h.shape[0]
    return pl.pallas_call(
        _in_proj_qk_kernel,
        grid=(2 * QKW // TN_IN, t // TM_IN),
        in_specs=[
            pl.BlockSpec((TM_IN, D_MODEL), lambda j, i: (i, 0)),
            pl.BlockSpec((None, D_MODEL, TN_IN), lambda j, i: (layer, 0, j)),
            pl.BlockSpec((TM_IN, DK), lambda j, i: (i, 0)),
            pl.BlockSpec((TM_IN, DK), lambda j, i: (i, 0)),
        ],
        out_specs=pl.BlockSpec((TM_IN, TN_IN), lambda j, i: (i, j)),
        out_shape=jax.ShapeDtypeStruct((t, 2 * QKW), BF16),
        scratch_shapes=[pltpu.VMEM((D_MODEL, TN_IN), BF16)],
        compiler_params=_params("arbitrary", "arbitrary"),
        name="in_proj_qk",
    )(h, w_in, cos_rows, sin_rows)


def _in_proj_act(h, w_in, layer, offsets, width, act):
    t = h.shape[0]
    per = width // TN_IN

    def src_tile(j):
        tile = offsets[0] // TN_IN
        for n, off in enumerate(offsets[1:], start=1):
            tile = jnp.where(j // per == n, off // TN_IN, tile)
        return tile + j % per

    return pl.pallas_call(
        functools.partial(_in_proj_act_kernel, act=act),
        grid=(len(offsets) * per, t // TM_IN),
        in_specs=[
            pl.BlockSpec((TM_IN, D_MODEL), lambda j, i: (i, 0)),
            pl.BlockSpec((None, D_MODEL, TN_IN), lambda j, i: (layer, 0, src_tile(j))),
        ],
        out_specs=pl.BlockSpec((TM_IN, TN_IN), lambda j, i: (i, j)),
        out_shape=jax.ShapeDtypeStruct((t, len(offsets) * width), BF16),
        scratch_shapes=[pltpu.VMEM((D_MODEL, TN_IN), BF16)],
        compiler_params=_params("arbitrary", "arbitrary"),
        name="in_proj_" + act,
    )(h, w_in)


def _retention_head(h, q_ref, k_ref, v_ref, gr_ref, mask_ref, qdec_ref, kdec_ref,
                    sdec_ref, s_ref, o_ref):
    ks = slice(h * DK, (h + 1) * DK)
    vs = slice(h * DV, (h + 1) * DV)
    qb = q_ref[:, ks]
    kb = k_ref[:, ks]
    vb = v_ref[:, vs]
    state = s_ref[0, h]

    scores = lax.dot_general(qb, kb, (((1,), (1,)), ((), ())),
                             preferred_element_type=F32)
    p = (scores * mask_ref[h]).astype(BF16)
    intra = jnp.dot(p, vb, preferred_element_type=F32)
    qd = (qb.astype(F32) * qdec_ref[h]).astype(BF16)
    cross = jnp.dot(qd, state.astype(BF16), preferred_element_type=F32)
    o = intra + cross

    kd = (kb.astype(F32) * kdec_ref[h]).astype(BF16)
    s_ref[0, h] = sdec_ref[h] * state + lax.dot_general(
        kd, vb, (((0,), (0,)), ((), ())), preferred_element_type=F32)

    mu = jnp.mean(o, axis=-1, keepdims=True)
    d = o - mu
    var = jnp.mean(d * d, axis=-1, keepdims=True)
    on = d * lax.rsqrt(var + EPS)
    o_ref[:, vs] = (gr_ref[:, vs].astype(F32) * on).astype(BF16)


def _build_mixing_matrix(ws_ref, w2_scr, ts, lg):
    i = lax.broadcasted_iota(jnp.int32, (lg, lg), 0)
    j = lax.broadcasted_iota(jnp.int32, (lg, lg), 1)
    keep = (i // CHUNK) >= (j // CHUNK)
    w2_scr[...] = jnp.zeros_like(w2_scr)
    for g in range(G_MLP):
        wg = jnp.where(keep, ws_ref[g, :lg, :lg], 0.0).astype(BF16)
        for c in range(ts // lg):
            w2_scr[g, c * lg:(c + 1) * lg, c * lg:(c + 1) * lg] = wg


def _gmlp_norm(vm_ref, lng_ref, lnb_ref):
    gv = vm_ref[...].astype(F32)
    mu = jnp.mean(gv, axis=-1, keepdims=True)
    d = gv - mu
    var = jnp.mean(d * d, axis=-1, keepdims=True)
    return d * lax.rsqrt(var + EPS) * lng_ref[0] + lnb_ref[0]


def _gmlp_group(g, vn, u_ref, gm_ref, bias_ref, w2_scr, m_ref):
    sl = slice(g * DG, (g + 1) * DG)
    s = jnp.dot(w2_scr[g], vn[:, sl].astype(BF16),
                preferred_element_type=F32) + bias_ref[:, g:g + 1]
    m = u_ref[:, sl].astype(F32) * s
    m_ref[:, sl] = (gm_ref[:, sl].astype(F32) * m).astype(BF16)


def _retention_tables(ts, chunk):
    log_gamma = jnp.log(1.0 - 2.0 ** (-5.0 - jnp.arange(H_R, dtype=F32)))
    idx = jnp.arange(ts, dtype=F32)
    diff = idx[:, None] - idx[None, :]
    cn = (jnp.arange(ts) // chunk)[:, None]
    cm = (jnp.arange(ts) // chunk)[None, :]
    expo = jnp.where(cn == cm, jnp.abs(diff), diff)
    mask = jnp.where((cn >= cm)[None],
                     jnp.exp(log_gamma[:, None, None] * expo[None]), 0.0)
    ones = jnp.ones((1, 1, DK), F32)
    qdec = jnp.exp(log_gamma[:, None] * (idx[None, :] + 1.0))[..., None] * ones
    kdec = jnp.exp(log_gamma[:, None] * (ts - 1.0 - idx[None, :]))[..., None] * ones
    sdec = jnp.exp(log_gamma * ts)[:, None, None] * jnp.ones((1, 1, DV), F32)
    return mask, qdec, kdec, sdec


def _rotary_tables(seq, pos0):
    half = DK // 2
    inv_freq = 1.0 / (ROPE_BASE ** (jnp.arange(half, dtype=F32) / half))
    pos = (pos0 + jnp.arange(seq)).astype(F32)
    ang = pos[:, None] * inv_freq[None, :]
    cos = jnp.cos(ang)
    sin = jnp.sin(ang)
    return (jnp.concatenate([cos, cos], axis=-1),
            jnp.concatenate([-sin, sin], axis=-1))


def _mixing_bias(ws_b, layer, ts, lg):
    return jnp.tile(jnp.transpose(ws_b[layer, :, :lg]), (ts // lg, 1))


def _mixer_kernel(q_ref, k_ref, v_ref, gr_ref, gm_ref, u_ref, vm_ref, ar_ref,
                  am_ref, mask_ref, qdec_ref, kdec_ref, sdec_ref, ws_ref,
                  bias_ref, lng_ref, lnb_ref, wr_ref, wm_ref, *rest, ts, lg, n_cast):
    w2_scr, r_scr, m_scr = rest[-3:]
    outs = rest[len(rest) - 3 - n_cast - 2:len(rest) - 3]
    merged_ref, s_ref = outs[:2]
    _cast_weight_slabs(rest[:n_cast], outs[2:])

    @pl.when((pl.program_id(0) == 0) & (pl.program_id(1) == 0))
    def _():
        _build_mixing_matrix(ws_ref, w2_scr, ts, lg)

    @pl.when(pl.program_id(1) == 0)
    def _():
        s_ref[...] = jnp.zeros_like(s_ref)

    for h in range(H_R):
        _retention_head(h, q_ref, k_ref, v_ref, gr_ref, mask_ref, qdec_ref,
                        kdec_ref, sdec_ref, s_ref, r_scr)
    rb = jnp.dot(r_scr[...], wr_ref[...], preferred_element_type=F32)
    vn = _gmlp_norm(vm_ref, lng_ref, lnb_ref)
    for g in range(G_MLP):
        _gmlp_group(g, vn, u_ref, gm_ref, bias_ref, w2_scr, m_scr)
    mb = jnp.dot(m_scr[...], wm_ref[...], preferred_element_type=F32)
    merged = ar_ref[...].astype(F32) * rb + am_ref[...].astype(F32) * mb
    merged_ref[...] = merged.astype(BF16)


def _mixer(qk, v, gates, uv, a_sig, tables, ws, ws_b, ln_g, ln_b, w_ret, w_mlp,
           layer, batch, seq, state_dest, cast_weights):
    mask, qdec, kdec, sdec = tables
    ts = TS_MIX
    lg = GMLP_CHUNK
    ns = seq // ts
    row = lambda b, s: b * ns + s
    const3 = lambda b, s: (0, 0, 0)
    bias = _mixing_bias(ws_b, layer, ts, lg)
    tile = lambda width, col: pl.BlockSpec((ts, width), lambda b, s: (row(b, s), col))
    w_in_specs, w_out_specs, w_out_shape = _cast_weight_specs(
        cast_weights, layer + 1, batch * ns, row)
    args = [qk, qk, v, gates, gates, uv, uv, a_sig, a_sig, mask, qdec, kdec, sdec,
            ws, bias, ln_g, ln_b, w_ret, w_mlp] + list(cast_weights)
    aliases = {}
    if state_dest is not None:
        args.append(state_dest)
        aliases = {len(args) - 1: 1}
    outs = pl.pallas_call(
        functools.partial(_mixer_kernel, ts=ts, lg=lg, n_cast=len(cast_weights)),
        grid=(batch, ns),
        in_specs=[
            tile(QKW, 0), tile(QKW, 1),
            tile(RV, 0),
            tile(RV, 0), tile(D_MLP, 1),
            tile(D_MLP, 0), tile(D_MLP, 1),
            tile(D_MODEL, 0), tile(D_MODEL, 1),
            _resident((H_R, ts, ts), const3),
            _resident((H_R, ts, DK), const3),
            _resident((H_R, ts, DK), const3),
            _resident((H_R, 1, DV), const3),
            _resident((None, G_MLP, GMLP_CHUNK, GMLP_CHUNK),
                      lambda b, s: (layer, 0, 0, 0)),
            _resident((ts, G_MLP), lambda b, s: (0, 0)),
            _resident((1, 1, D_MLP), lambda b, s: (layer, 0, 0)),
            _resident((1, 1, D_MLP), lambda b, s: (layer, 0, 0)),
            _resident((RV, D_MODEL), lambda b, s: (0, 0)),
            _resident((D_MLP, D_MODEL), lambda b, s: (0, 0)),
        ] + w_in_specs + ([_ANY] if state_dest is not None else []),
        out_specs=[
            pl.BlockSpec((ts, D_MODEL), lambda b, s: (row(b, s), 0)),
            pl.BlockSpec((None, 1, H_R, DK, DV), lambda b, s: (layer, b, 0, 0, 0)),
        ] + w_out_specs,
        out_shape=[
            jax.ShapeDtypeStruct((batch * seq, D_MODEL), BF16),
            jax.ShapeDtypeStruct((DEPTH, batch, H_R, DK, DV), F32),
        ] + w_out_shape,
        input_output_aliases=aliases,
        scratch_shapes=[pltpu.VMEM((G_MLP, ts, ts), BF16),
                        pltpu.VMEM((ts, RV), BF16),
                        pltpu.VMEM((ts, D_MLP), BF16)],
        compiler_params=_params("arbitrary", "arbitrary"),
        name="mixer",
    )(*args)
    return outs[0], outs[1], tuple(outs[2:])


def _out_proj_kernel(mg_ref, x_ref, wo_ref, g_ref, *outs, final_norm):
    for c in range(TM_OUT // RC):
        rows = slice(c * RC, (c + 1) * RC)
        y = x_ref[rows, :] + jnp.dot(mg_ref[rows, :], wo_ref[...],
                                     preferred_element_type=F32)
        if final_norm:
            outs[0][rows, :] = _rmsnorm(y, g_ref[0])
        else:
            outs[0][rows, :] = y
            outs[1][rows, :] = _rmsnorm(y, g_ref[0]).astype(BF16)


def _out_proj(merged, x, w_o, g, g_layer, total_rows, final_norm):
    rows = merged.shape[0]
    tm = TM_OUT
    out_block = pl.BlockSpec((tm, D_MODEL), lambda i: (i, 0))
    if final_norm:
        out_specs = out_block
        out_shape = jax.ShapeDtypeStruct((rows, D_MODEL), F32)
    else:
        out_specs = [out_block, out_block]
        out_shape = [jax.ShapeDtypeStruct((total_rows, D_MODEL), F32),
                     jax.ShapeDtypeStruct((total_rows, D_MODEL), BF16)]
    return pl.pallas_call(
        functools.partial(_out_proj_kernel, final_norm=final_norm),
        grid=(rows // tm,),
        in_specs=[
            pl.BlockSpec((tm, D_MODEL), lambda i: (i, 0)),
            pl.BlockSpec((tm, D_MODEL), lambda i: (i, 0)),
            _resident((D_MODEL, D_MODEL), lambda i: (0, 0)),
            pl.BlockSpec((1, 1, D_MODEL), lambda i: (g_layer, 0, 0)),
        ],
        out_specs=out_specs,
        out_shape=out_shape,
        compiler_params=_params("parallel"),
        name="out_proj",
    )(merged, x, w_o, g)


def _retention_kernel(q_ref, k_ref, v_ref, gr_ref, mask_ref, qdec_ref, kdec_ref,
                      sdec_ref, s0_ref, *rest):
    o_ref, s_ref = rest[-2:]

    @pl.when(pl.program_id(1) == 0)
    def _():
        s_ref[...] = s0_ref[...]

    for h in range(H_R):
        _retention_head(h, q_ref, k_ref, v_ref, gr_ref, mask_ref, qdec_ref,
                        kdec_ref, sdec_ref, s_ref, o_ref)


def _retention(qk, v, gates, s0_all, tables, layer, row0, batch, seq, ts, state_dest):
    mask, qdec, kdec, sdec = tables
    ns = seq // ts
    r0 = row0 // ts
    row = lambda b, s: r0 + b * ns + s
    const3 = lambda b, s: (0, 0, 0)
    state_block = pl.BlockSpec((None, 1, H_R, DK, DV), lambda b, s: (layer, b, 0, 0, 0))
    args = [qk, qk, v, gates, mask, qdec, kdec, sdec, s0_all]
    aliases = {}
    if state_dest is not None:
        args.append(state_dest)
        aliases = {len(args) - 1: 1}
    return pl.pallas_call(
        _retention_kernel,
        grid=(batch, ns),
        in_specs=[
            pl.BlockSpec((ts, QKW), lambda b, s: (row(b, s), 0)),
            pl.BlockSpec((ts, QKW), lambda b, s: (row(b, s), 1)),
            pl.BlockSpec((ts, RV), lambda b, s: (row(b, s), 0)),
            pl.BlockSpec((ts, RV), lambda b, s: (row(b, s), 0)),
            pl.BlockSpec((H_R, ts, ts), const3),
            pl.BlockSpec((H_R, ts, DK), const3),
            pl.BlockSpec((H_R, ts, DK), const3),
            pl.BlockSpec((H_R, 1, DV), const3),
            state_block,
        ] + ([_ANY] if state_dest is not None else []),
        out_specs=[
            pl.BlockSpec((ts, RV), lambda b, s: (b * ns + s, 0)),
            state_block,
        ],
        out_shape=[
            jax.ShapeDtypeStruct((batch * seq, RV), BF16),
            jax.ShapeDtypeStruct(s0_all.shape, F32),
        ],
        input_output_aliases=aliases,
        compiler_params=_params("parallel", "arbitrary"),
        name="retention",
    )(*args)


def _gmlp_kernel(u_ref, vm_ref, gm_ref, ws_ref, bias_ref, lng_ref, lnb_ref,
                 *rest, ts, lg):
    m_ref, vn_ref, w2_scr = rest[-3:]

    @pl.when((pl.program_id(0) == 0) & (pl.program_id(1) == 0))
    def _():
        _build_mixing_matrix(ws_ref, w2_scr, ts, lg)

    vn = _gmlp_norm(vm_ref, lng_ref, lnb_ref)
    vn_ref[...] = vn
    for g in range(G_MLP):
        _gmlp_group(g, vn, u_ref, gm_ref, bias_ref, w2_scr, m_ref)


def _gmlp(uv, gates, ws, ws_b, ln_g, ln_b, layer, row0, batch, seq, ts, vn_dest):
    lg = min(GMLP_CHUNK, seq)
    ns = seq // ts
    t = batch * seq
    r0 = row0 // ts
    row = lambda b, s: r0 + b * ns + s
    args = [uv, uv, gates, ws, _mixing_bias(ws_b, layer, ts, lg), ln_g, ln_b]
    aliases = {}
    if vn_dest is not None:
        args.append(vn_dest)
        aliases = {len(args) - 1: 1}
    return pl.pallas_call(
        functools.partial(_gmlp_kernel, ts=ts, lg=lg),
        grid=(batch, ns),
        in_specs=[
            pl.BlockSpec((ts, D_MLP), lambda b, s: (row(b, s), 0)),
            pl.BlockSpec((ts, D_MLP), lambda b, s: (row(b, s), 1)),
            pl.BlockSpec((ts, D_MLP), lambda b, s: (row(b, s), 1)),
            pl.BlockSpec((None, G_MLP, GMLP_CHUNK, GMLP_CHUNK),
                         lambda b, s: (layer, 0, 0, 0)),
            pl.BlockSpec((ts, G_MLP), lambda b, s: (0, 0)),
            pl.BlockSpec((1, 1, D_MLP), lambda b, s: (layer, 0, 0)),
            pl.BlockSpec((1, 1, D_MLP), lambda b, s: (layer, 0, 0)),
        ] + ([_ANY] if vn_dest is not None else []),
        out_specs=[
            pl.BlockSpec((ts, D_MLP), lambda b, s: (b * ns + s, 0)),
            pl.BlockSpec((None, ts, D_MLP), lambda b, s: (layer, b * ns + s, 0)),
        ],
        out_shape=[jax.ShapeDtypeStruct((t, D_MLP), BF16),
                   jax.ShapeDtypeStruct((DEPTH, t, D_MLP), F32)],
        input_output_aliases=aliases,
        scratch_shapes=[pltpu.VMEM((G_MLP, ts, ts), BF16)],
        compiler_params=_params("arbitrary", "arbitrary"),
        name="gmlp",
    )(*args)


def _merge_out_kernel(r_ref, m_ref, ar_ref, am_ref, x_ref, wr_ref, wm_ref,
                      wo_ref, g_ref, *rest, final_norm):
    outs = rest[-1:] if final_norm else rest[-2:]
    rb = jnp.dot(r_ref[...], wr_ref[...], preferred_element_type=F32)
    mb = jnp.dot(m_ref[...], wm_ref[...], preferred_element_type=F32)
    merged = ar_ref[...].astype(F32) * rb + am_ref[...].astype(F32) * mb
    y = x_ref[...] + jnp.dot(merged.astype(BF16), wo_ref[...],
                             preferred_element_type=F32)
    if final_norm:
        outs[0][...] = _rmsnorm(y, g_ref[0])
    else:
        outs[0][...] = y
        outs[1][...] = _rmsnorm(y, g_ref[0]).astype(BF16)


def _merge_out(r, m, a_sig, x, x_row0, w_ret, w_mlp, w_o, g, g_layer,
               row0, dest, final_norm):
    tm = r.shape[0]
    r0 = row0 // tm
    xr0 = x_row0 // tm
    weight = lambda: _resident((D_MODEL, D_MODEL), lambda i: (0, 0))
    in_specs = [
        pl.BlockSpec((tm, RV), lambda i: (i, 0)),
        pl.BlockSpec((tm, D_MLP), lambda i: (i, 0)),
        pl.BlockSpec((tm, D_MODEL), lambda i: (r0 + i, 0)),
        pl.BlockSpec((tm, D_MODEL), lambda i: (r0 + i, 1)),
        pl.BlockSpec((tm, D_MODEL), lambda i: (xr0 + i, 0)),
        weight(), weight(), weight(),
        pl.BlockSpec((1, 1, D_MODEL), lambda i: (g_layer, 0, 0)),
    ]
    args = [r, m, a_sig, a_sig, x, w_ret, w_mlp, w_o, g]
    if final_norm:
        out_specs = pl.BlockSpec((tm, D_MODEL), lambda i: (i, 0))
        out_shape = jax.ShapeDtypeStruct((tm, D_MODEL), F32)
        aliases = {}
    else:
        out_block = pl.BlockSpec((tm, D_MODEL), lambda i: (r0 + i, 0))
        out_specs = [out_block, out_block]
        out_shape = [jax.ShapeDtypeStruct(dest[0].shape, F32),
                     jax.ShapeDtypeStruct(dest[1].shape, BF16)]
        in_specs += [_ANY, _ANY]
        args += list(dest)
        aliases = {len(args) - 2: 0, len(args) - 1: 1}
    return pl.pallas_call(
        functools.partial(_merge_out_kernel, final_norm=final_norm),
        grid=(1,),
        in_specs=in_specs,
        out_specs=out_specs,
        out_shape=out_shape,
        input_output_aliases=aliases,
        compiler_params=_params("arbitrary"),
        name="merge_out",
    )(*args)


def kernel(x_prompt, x_sample, state_ret, norm_g, w_in, ws, ws_b, ln_g, ln_b,
           w_ret_out, w_mlp_out, w_o, final_g):
    b_p, s_p, _ = x_prompt.shape
    b_s, s_s, _ = x_sample.shape
    t_p = b_p * s_p
    t_s = b_s * s_s
    t_all = t_p + t_s
    tab_p = _retention_tables(TS_MIX, CHUNK)
    tab_s = _retention_tables(s_s, s_s)
    cos_p, sin_p = _rotary_tables(s_p, 0)
    cos_s, sin_s = _rotary_tables(s_s, PAST_LEN)
    cos_rows = jnp.concatenate([jnp.tile(cos_p, (b_p, 1)), jnp.tile(cos_s, (b_s, 1))])
    sin_rows = jnp.concatenate([jnp.tile(sin_p, (b_p, 1)), jnp.tile(sin_s, (b_s, 1))])
    norm_g3 = norm_g.reshape(DEPTH, 1, D_MODEL)
    ln_g3 = ln_g.reshape(DEPTH, 1, D_MLP)
    ln_b3 = ln_b.reshape(DEPTH, 1, D_MLP)
    fg3 = final_g.reshape(1, 1, D_MODEL)
    out_weights = (w_ret_out, w_mlp_out, w_o)

    x_p, x_s, xs_row0 = x_prompt.reshape(t_p, D_MODEL), x_sample.reshape(t_s, D_MODEL), 0
    h, wb = _prenorm(x_p, norm_g3, 0, TM_OUT, t_all, 0, cast_weights=out_weights)
    h, _ = _prenorm(x_s, norm_g3, 0, t_s, t_all, t_p, dest=h)
    st_p = st_s = v_s = None
    for l in range(DEPTH):
        last = l == DEPTH - 1
        qk = _in_proj_qk(h, w_in, l, cos_rows, sin_rows)
        v = _in_proj_act(h, w_in, l, (OFF_V,), RV, "copy")
        gates = _in_proj_act(h, w_in, l, (OFF_GR, OFF_GM), RV, "silu")
        uv = _in_proj_act(h, w_in, l, (OFF_U, OFF_VM), D_MLP, "gelu")
        a_sig = _in_proj_act(h, w_in, l, (OFF_AR, OFF_AM), D_MODEL, "sigmoid")

        w_ret_b, w_mlp_b, w_o_b = wb
        merged_p, st_p, wb = _mixer(qk, v, gates, uv, a_sig, tab_p, ws, ws_b, ln_g3,
                                    ln_b3, w_ret_b, w_mlp_b, l, b_p, s_p, st_p,
                                    () if last else out_weights)
        r_s, st_s = _retention(qk, v, gates, state_ret, tab_s, l, t_p, b_s, s_s, s_s,
                               st_s)
        m_s, v_s = _gmlp(uv, gates, ws, ws_b, ln_g3, ln_b3, l, t_p, b_s, s_s, s_s, v_s)

        g, g_layer = (fg3, 0) if last else (norm_g3, l + 1)
        out_p = _out_proj(merged_p, x_p, w_o_b, g, g_layer, t_all, last)
        out_s = _merge_out(r_s, m_s, a_sig, x_s, xs_row0, w_ret_b, w_mlp_b, w_o_b,
                           g, g_layer, t_p, None if last else out_p, last)
        if last:
            y_p, y_s = out_p, out_s
        else:
            x_all, h = out_s
            x_p, x_s, xs_row0 = x_all, x_all, t_p
    return (y_p.reshape(b_p, s_p, D_MODEL), y_s.reshape(b_s, s_s, D_MODEL),
            st_p, st_s, v_s.reshape(DEPTH, b_s, s_s, D_MLP))
```

```python
import functools

import jax
import jax.numpy as jnp
import numpy as np
from jax import lax
from jax.experimental import pallas as pl
from jax.experimental.pallas import tpu as pltpu

D_MODEL = 2048
DEPTH = 4
PAST_LEN = 1024
CHUNK = 64
H_R = 8
DK = D_MODEL // 16
DV = D_MODEL // 8
QKW = H_R * DK
RV = H_R * DV
D_MLP = D_MODEL
G_MLP = 8
DG = D_MLP // G_MLP
GMLP_CHUNK = 128
ROPE_BASE = 10000.0
EPS = 1e-6
N_IN = QKW * 2 + RV * 2 + D_MLP * 3 + D_MODEL * 2

OFF_Q = 0
OFF_K = OFF_Q + QKW
OFF_V = OFF_K + QKW
OFF_GR = OFF_V + RV
OFF_U = OFF_GR + RV
OFF_VM = OFF_U + D_MLP
OFF_GM = OFF_VM + D_MLP
OFF_AR = OFF_GM + D_MLP
OFF_AM = OFF_AR + D_MODEL

VMEM_LIMIT = 56 * 1024 * 1024

TM_IN = 2080
RC_IN = 208
TN_IN = 1024
RC = 256
TS_MIX = 256
TM_OUT = 512

F32 = jnp.float32
BF16 = jnp.bfloat16


def _rmsnorm(x, g):
    return x * lax.rsqrt(jnp.mean(x * x, axis=-1, keepdims=True) + EPS) * g


def _params(*sem):
    return pltpu.CompilerParams(dimension_semantics=sem,
                                vmem_limit_bytes=VMEM_LIMIT)


def _resident(block_shape, index_map):
    return pl.BlockSpec(block_shape, index_map, pipeline_mode=pl.Buffered(1))


_ANY = pl.BlockSpec(memory_space=pl.ANY)


def _cast_weight_specs(weights, layer, n_steps, step_of):
    rows = D_MODEL // n_steps
    in_specs = [pl.BlockSpec((None, rows, D_MODEL),
                             lambda *ids: (layer, step_of(*ids), 0)) for _ in weights]
    out_specs = [pl.BlockSpec((rows, D_MODEL), lambda *ids: (step_of(*ids), 0))
                 for _ in weights]
    out_shape = [jax.ShapeDtypeStruct((D_MODEL, D_MODEL), BF16) for _ in weights]
    return in_specs, out_specs, out_shape


def _cast_weight_slabs(w_refs, wb_refs):
    for w_ref, wb_ref in zip(w_refs, wb_refs):
        wb_ref[...] = w_ref[...].astype(BF16)


def _prenorm_kernel(x_ref, g_ref, *rest, n_cast):
    h_ref = rest[len(rest) - n_cast - 1]
    h_ref[...] = _rmsnorm(x_ref[...], g_ref[0]).astype(BF16)
    _cast_weight_slabs(rest[:n_cast], rest[len(rest) - n_cast:])


def _prenorm(x, norm_g, layer, tm, total_rows, row0, dest=None, cast_weights=()):
    r0 = row0 // tm
    n_steps = x.shape[0] // tm
    w_in_specs, w_out_specs, w_out_shape = _cast_weight_specs(
        cast_weights, layer, n_steps, lambda i: i)
    in_specs = [
        pl.BlockSpec((tm, D_MODEL), lambda i: (i, 0)),
        pl.BlockSpec((1, 1, D_MODEL), lambda i: (layer, 0, 0)),
    ] + w_in_specs
    args = [x, norm_g] + list(cast_weights)
    aliases = {}
    if dest is not None:
        in_specs.append(_ANY)
        args.append(dest)
        aliases = {len(args) - 1: 0}
    outs = pl.pallas_call(
        functools.partial(_prenorm_kernel, n_cast=len(cast_weights)),
        grid=(n_steps,),
        in_specs=in_specs,
        out_specs=[pl.BlockSpec((tm, D_MODEL), lambda i: (r0 + i, 0))] + w_out_specs,
        out_shape=[jax.ShapeDtypeStruct((total_rows, D_MODEL), BF16)] + w_out_shape,
        input_output_aliases=aliases,
        compiler_params=_params("parallel"),
        name="prenorm",
    )(*args)
    return outs[0], tuple(outs[1:])


def _cast_weight_tile(w_ref, wb_scr):
    @pl.when(pl.program_id(1) == 0)
    def _():
        wb_scr[...] = w_ref[...].astype(BF16)


def _activation(act, x):
    if act == "copy":
        return x
    if act == "sigmoid":
        return jax.nn.sigmoid(x)
    if act == "silu":
        return x * jax.nn.sigmoid(x)
    if act == "gelu":
        return 0.5 * x * (1.0 + lax.erf(x * np.float32(np.sqrt(0.5))))
    raise ValueError(act)


def _section_acts():
    kinds = []
    for width, act in ((QKW, "rotary"), (QKW, "rotary"), (RV, "copy"),
                       (RV, "silu"), (D_MLP, "gelu"), (D_MLP, "gelu"),
                       (D_MLP, "silu"), (D_MODEL, "sigmoid"), (D_MODEL, "sigmoid")):
        kinds += [act] * (width // TN_IN)
    return tuple(kinds)


def _in_proj_kernel(h_ref, w_ref, cos_ref, sin_ref, z_ref, wb_scr):
    _cast_weight_tile(w_ref, wb_scr)
    j = pl.program_id(0)
    tile_acts = _section_acts()

    def body(act):
        for c in range(TM_IN // RC_IN):
            rows = slice(c * RC_IN, (c + 1) * RC_IN)
            x = jnp.dot(h_ref[rows, :], wb_scr[...], preferred_element_type=F32)
            if act == "rotary":
                scale = jnp.where(j >= QKW // TN_IN, np.float32(DK ** -0.5),
                                  np.float32(1.0))
                cos = cos_ref[rows, :]
                sin = sin_ref[rows, :]
                for h in range(TN_IN // DK):
                    ks = slice(h * DK, (h + 1) * DK)
                    xs = x[:, ks]
                    r = xs * cos + pltpu.roll(xs, DK // 2, 1) * sin
                    z_ref[rows, ks] = (r * scale).astype(BF16)
            else:
                z_ref[rows, :] = _activation(act, x).astype(BF16)

    for act in sorted(set(tile_acts)):
        is_act = functools.reduce(
            lambda a, b: a | b, [j == jj for jj, a in enumerate(tile_acts) if a == act])
        pl.when(is_act)(functools.partial(body, act))


def _in_proj(h, w_in, layer, cos_rows, sin_rows):
    t = h.shape[0]
    table_row = lambda j, i: (jnp.where(j < 2 * QKW // TN_IN, i, 0), 0)
    return pl.pallas_call(
        _in_proj_kernel,
        grid=(N_IN // TN_IN, t // TM_IN),
        in_specs=[
            pl.BlockSpec((TM_IN, D_MODEL), lambda j, i: (i, 0)),
            pl.BlockSpec((None, D_MODEL, TN_IN), lambda j, i: (layer, 0, j)),
            pl.BlockSpec((TM_IN, DK), table_row),
            pl.BlockSpec((TM_IN, DK), table_row),
        ],
        out_specs=pl.BlockSpec((TM_IN, TN_IN), lambda j, i: (i, j)),
        out_shape=jax.ShapeDtypeStruct((t, N_IN), BF16),
        scratch_shapes=[pltpu.VMEM((D_MODEL, TN_IN), BF16)],
        compiler_params=_params("arbitrary", "arbitrary"),
        name="in_proj",
    )(h, w_in, cos_rows, sin_rows)


def _retention_head(h, q_ref, k_ref, v_ref, gr_ref, mask_ref, qdec_ref, kdec_ref,
                    sdec_ref, s_ref, o_ref):
    ks = slice(h * DK, (h + 1) * DK)
    vs = slice(h * DV, (h + 1) * DV)
    qb = q_ref[:, ks]
    kb = k_ref[:, ks]
    vb = v_ref[:, vs]
    state = s_ref[0, h]

    scores = lax.dot_general(qb, kb, (((1,), (1,)), ((), ())),
                             preferred_element_type=F32)
    p = (scores * mask_ref[h]).astype(BF16)
    intra = jnp.dot(p, vb, preferred_element_type=F32)
    qd = (qb.astype(F32) * qdec_ref[h]).astype(BF16)
    cross = jnp.dot(qd, state.astype(BF16), preferred_element_type=F32)
    o = intra + cross

    kd = (kb.astype(F32) * kdec_ref[h]).astype(BF16)
    s_ref[0, h] = sdec_ref[h] * state + lax.dot_general(
        kd, vb, (((0,), (0,)), ((), ())), preferred_element_type=F32)

    mu = jnp.mean(o, axis=-1, keepdims=True)
    d = o - mu
    var = jnp.mean(d * d, axis=-1, keepdims=True)
    on = d * lax.rsqrt(var + EPS)
    o_ref[:, vs] = (gr_ref[:, vs].astype(F32) * on).astype(BF16)


def _build_mixing_matrix(ws_ref, w2_scr, ts, lg):
    i = lax.broadcasted_iota(jnp.int32, (lg, lg), 0)
    j = lax.broadcasted_iota(jnp.int32, (lg, lg), 1)
    keep = (i // CHUNK) >= (j // CHUNK)
    w2_scr[...] = jnp.zeros_like(w2_scr)
    for g in range(G_MLP):
        wg = jnp.where(keep, ws_ref[g, :lg, :lg], 0.0).astype(BF16)
        for c in range(ts // lg):
            w2_scr[g, c * lg:(c + 1) * lg, c * lg:(c + 1) * lg] = wg


def _gmlp_norm(vm_ref, lng_ref, lnb_ref):
    gv = vm_ref[...].astype(F32)
    mu = jnp.mean(gv, axis=-1, keepdims=True)
    d = gv - mu
    var = jnp.mean(d * d, axis=-1, keepdims=True)
    return d * lax.rsqrt(var + EPS) * lng_ref[0] + lnb_ref[0]


def _gmlp_group(g, vn, u_ref, gm_ref, bias_ref, w2_scr, m_ref):
    sl = slice(g * DG, (g + 1) * DG)
    s = jnp.dot(w2_scr[g], vn[:, sl].astype(BF16),
                preferred_element_type=F32) + bias_ref[:, g:g + 1]
    m = u_ref[:, sl].astype(F32) * s
    m_ref[:, sl] = (gm_ref[:, sl].astype(F32) * m).astype(BF16)


def _retention_tables(ts, chunk):
    log_gamma = jnp.log(1.0 - 2.0 ** (-5.0 - jnp.arange(H_R, dtype=F32)))
    idx = jnp.arange(ts, dtype=F32)
    diff = idx[:, None] - idx[None, :]
    cn = (jnp.arange(ts) // chunk)[:, None]
    cm = (jnp.arange(ts) // chunk)[None, :]
    expo = jnp.where(cn == cm, jnp.abs(diff), diff)
    mask = jnp.where((cn >= cm)[None],
                     jnp.exp(log_gamma[:, None, None] * expo[None]), 0.0)
    ones = jnp.ones((1, 1, DK), F32)
    qdec = jnp.exp(log_gamma[:, None] * (idx[None, :] + 1.0))[..., None] * ones
    kdec = jnp.exp(log_gamma[:, None] * (ts - 1.0 - idx[None, :]))[..., None] * ones
    sdec = jnp.exp(log_gamma * ts)[:, None, None] * jnp.ones((1, 1, DV), F32)
    return mask, qdec, kdec, sdec


def _rotary_tables(seq, pos0):
    half = DK // 2
    inv_freq = 1.0 / (ROPE_BASE ** (jnp.arange(half, dtype=F32) / half))
    pos = (pos0 + jnp.arange(seq)).astype(F32)
    ang = pos[:, None] * inv_freq[None, :]
    cos = jnp.cos(ang)
    sin = jnp.sin(ang)
    return (jnp.concatenate([cos, cos], axis=-1),
            jnp.concatenate([-sin, sin], axis=-1))


def _mixing_bias(ws_b, layer, ts, lg):
    return jnp.tile(jnp.transpose(ws_b[layer, :, :lg]), (ts // lg, 1))


def _mixer_kernel(q_ref, k_ref, v_ref, gr_ref, gm_ref, u_ref, vm_ref, ar_ref,
                  am_ref, mask_ref, qdec_ref, kdec_ref, sdec_ref, ws_ref,
                  bias_ref, lng_ref, lnb_ref, wr_ref, wm_ref, *rest, ts, lg, n_cast):
    w2_scr, r_scr, m_scr = rest[-3:]
    outs = rest[len(rest) - 3 - n_cast - 2:len(rest) - 3]
    merged_ref, s_ref = outs[:2]
    _cast_weight_slabs(rest[:n_cast], outs[2:])

    @pl.when((pl.program_id(0) == 0) & (pl.program_id(1) == 0))
    def _():
        _build_mixing_matrix(ws_ref, w2_scr, ts, lg)

    @pl.when(pl.program_id(1) == 0)
    def _():
        s_ref[...] = jnp.zeros_like(s_ref)

    for h in range(H_R):
        _retention_head(h, q_ref, k_ref, v_ref, gr_ref, mask_ref, qdec_ref,
                        kdec_ref, sdec_ref, s_ref, r_scr)
    rb = jnp.dot(r_scr[...], wr_ref[...], preferred_element_type=F32)
    vn = _gmlp_norm(vm_ref, lng_ref, lnb_ref)
    for g in range(G_MLP):
        _gmlp_group(g, vn, u_ref, gm_ref, bias_ref, w2_scr, m_scr)
    mb = jnp.dot(m_scr[...], wm_ref[...], preferred_element_type=F32)
    merged = ar_ref[...].astype(F32) * rb + am_ref[...].astype(F32) * mb
    merged_ref[...] = merged.astype(BF16)


def _mixer(z, tables, ws, ws_b, ln_g, ln_b, w_ret, w_mlp,
           layer, batch, seq, state_dest, cast_weights):
    mask, qdec, kdec, sdec = tables
    ts = TS_MIX
    lg = GMLP_CHUNK
    ns = seq // ts
    row = lambda b, s: b * ns + s
    const3 = lambda b, s: (0, 0, 0)
    bias = _mixing_bias(ws_b, layer, ts, lg)
    tile = lambda width, col: pl.BlockSpec((ts, width), lambda b, s: (row(b, s), col))
    w_in_specs, w_out_specs, w_out_shape = _cast_weight_specs(
        cast_weights, layer + 1, batch * ns, row)
    args = [z] * 9 + [mask, qdec, kdec, sdec,
                      ws, bias, ln_g, ln_b, w_ret, w_mlp] + list(cast_weights)
    aliases = {}
    if state_dest is not None:
        args.append(state_dest)
        aliases = {len(args) - 1: 1}
    outs = pl.pallas_call(
        functools.partial(_mixer_kernel, ts=ts, lg=lg, n_cast=len(cast_weights)),
        grid=(batch, ns),
        in_specs=[
            tile(QKW, OFF_Q // QKW), tile(QKW, OFF_K // QKW),
            tile(RV, OFF_V // RV),
            tile(RV, OFF_GR // RV), tile(D_MLP, OFF_GM // D_MLP),
            tile(D_MLP, OFF_U // D_MLP), tile(D_MLP, OFF_VM // D_MLP),
            tile(D_MODEL, OFF_AR // D_MODEL), tile(D_MODEL, OFF_AM // D_MODEL),
            _resident((H_R, ts, ts), const3),
            _resident((H_R, ts, DK), const3),
            _resident((H_R, ts, DK), const3),
            _resident((H_R, 1, DV), const3),
            _resident((None, G_MLP, GMLP_CHUNK, GMLP_CHUNK),
                      lambda b, s: (layer, 0, 0, 0)),
            _resident((ts, G_MLP), lambda b, s: (0, 0)),
            _resident((1, 1, D_MLP), lambda b, s: (layer, 0, 0)),
            _resident((1, 1, D_MLP), lambda b, s: (layer, 0, 0)),
            _resident((RV, D_MODEL), lambda b, s: (0, 0)),
            _resident((D_MLP, D_MODEL), lambda b, s: (0, 0)),
        ] + w_in_specs + ([_ANY] if state_dest is not None else []),
        out_specs=[
            pl.BlockSpec((ts, D_MODEL), lambda b, s: (row(b, s), 0)),
            pl.BlockSpec((None, 1, H_R, DK, DV), lambda b, s: (layer, b, 0, 0, 0)),
        ] + w_out_specs,
        out_shape=[
            jax.ShapeDtypeStruct((batch * seq, D_MODEL), BF16),
            jax.ShapeDtypeStruct((DEPTH, batch, H_R, DK, DV), F32),
        ] + w_out_shape,
        input_output_aliases=aliases,
        scratch_shapes=[pltpu.VMEM((G_MLP, ts, ts), BF16),
                        pltpu.VMEM((ts, RV), BF16),
                        pltpu.VMEM((ts, D_MLP), BF16)],
        compiler_params=_params("arbitrary", "arbitrary"),
        name="mixer",
    )(*args)
    return outs[0], outs[1], tuple(outs[2:])


def _out_proj_kernel(mg_ref, x_ref, wo_ref, g_ref, *outs, final_norm):
    for c in range(TM_OUT // RC):
        rows = slice(c * RC, (c + 1) * RC)
        y = x_ref[rows, :] + jnp.dot(mg_ref[rows, :], wo_ref[...],
                                     preferred_element_type=F32)
        if final_norm:
            outs[0][rows, :] = _rmsnorm(y, g_ref[0])
        else:
            outs[0][rows, :] = y
            outs[1][rows, :] = _rmsnorm(y, g_ref[0]).astype(BF16)


def _out_proj(merged, x, w_o, g, g_layer, total_rows, final_norm):
    rows = merged.shape[0]
    tm = TM_OUT
    out_block = pl.BlockSpec((tm, D_MODEL), lambda i: (i, 0))
    if final_norm:
        out_specs = out_block
        out_shape = jax.ShapeDtypeStruct((rows, D_MODEL), F32)
    else:
        out_specs = [out_block, out_block]
        out_shape = [jax.ShapeDtypeStruct((total_rows, D_MODEL), F32),
                     jax.ShapeDtypeStruct((total_rows, D_MODEL), BF16)]
    return pl.pallas_call(
        functools.partial(_out_proj_kernel, final_norm=final_norm),
        grid=(rows // tm,),
        in_specs=[
            pl.BlockSpec((tm, D_MODEL), lambda i: (i, 0)),
            pl.BlockSpec((tm, D_MODEL), lambda i: (i, 0)),
            _resident((D_MODEL, D_MODEL), lambda i: (0, 0)),
            pl.BlockSpec((1, 1, D_MODEL), lambda i: (g_layer, 0, 0)),
        ],
        out_specs=out_specs,
        out_shape=out_shape,
        compiler_params=_params("parallel"),
        name="out_proj",
    )(merged, x, w_o, g)


def _retention_kernel(q_ref, k_ref, v_ref, gr_ref, mask_ref, qdec_ref, kdec_ref,
                      sdec_ref, s0_ref, *rest):
    o_ref, s_ref = rest[-2:]

    @pl.when(pl.program_id(1) == 0)
    def _():
        s_ref[...] = s0_ref[...]

    for h in range(H_R):
        _retention_head(h, q_ref, k_ref, v_ref, gr_ref, mask_ref, qdec_ref,
                        kdec_ref, sdec_ref, s_ref, o_ref)


def _retention(z, s0_all, tables, layer, row0, batch, seq, ts, state_dest):
    mask, qdec, kdec, sdec = tables
    ns = seq // ts
    r0 = row0 // ts
    row = lambda b, s: r0 + b * ns + s
    const3 = lambda b, s: (0, 0, 0)
    state_block = pl.BlockSpec((None, 1, H_R, DK, DV), lambda b, s: (layer, b, 0, 0, 0))
    args = [z, z, z, z, mask, qdec, kdec, sdec, s0_all]
    aliases = {}
    if state_dest is not None:
        args.append(state_dest)
        aliases = {len(args) - 1: 1}
    return pl.pallas_call(
        _retention_kernel,
        grid=(batch, ns),
        in_specs=[
            pl.BlockSpec((ts, QKW), lambda b, s: (row(b, s), OFF_Q // QKW)),
            pl.BlockSpec((ts, QKW), lambda b, s: (row(b, s), OFF_K // QKW)),
            pl.BlockSpec((ts, RV), lambda b, s: (row(b, s), OFF_V // RV)),
            pl.BlockSpec((ts, RV), lambda b, s: (row(b, s), OFF_GR // RV)),
            pl.BlockSpec((H_R, ts, ts), const3),
            pl.BlockSpec((H_R, ts, DK), const3),
            pl.BlockSpec((H_R, ts, DK), const3),
            pl.BlockSpec((H_R, 1, DV), const3),
            state_block,
        ] + ([_ANY] if state_dest is not None else []),
        out_specs=[
            pl.BlockSpec((ts, RV), lambda b, s: (b * ns + s, 0)),
            state_block,
        ],
        out_shape=[
            jax.ShapeDtypeStruct((batch * seq, RV), BF16),
            jax.ShapeDtypeStruct(s0_all.shape, F32),
        ],
        input_output_aliases=aliases,
        compiler_params=_params("parallel", "arbitrary"),
        name="retention",
    )(*args)


def _gmlp_kernel(u_ref, vm_ref, gm_ref, ws_ref, bias_ref, lng_ref, lnb_ref,
                 *rest, ts, lg):
    m_ref, vn_ref, w2_scr = rest[-3:]

    @pl.when((pl.program_id(0) == 0) & (pl.program_id(1) == 0))
    def _():
        _build_mixing_matrix(ws_ref, w2_scr, ts, lg)

    vn = _gmlp_norm(vm_ref, lng_ref, lnb_ref)
    vn_ref[...] = vn
    for g in range(G_MLP):
        _gmlp_group(g, vn, u_ref, gm_ref, bias_ref, w2_scr, m_ref)


def _gmlp(z, ws, ws_b, ln_g, ln_b, layer, row0, batch, seq, ts, vn_dest):
    lg = min(GMLP_CHUNK, seq)
    ns = seq // ts
    t = batch * seq
    r0 = row0 // ts
    row = lambda b, s: r0 + b * ns + s
    args = [z, z, z, ws, _mixing_bias(ws_b, layer, ts, lg), ln_g, ln_b]
    aliases = {}
    if vn_dest is not None:
        args.append(vn_dest)
        aliases = {len(args) - 1: 1}
    return pl.pallas_call(
        functools.partial(_gmlp_kernel, ts=ts, lg=lg),
        grid=(batch, ns),
        in_specs=[
            pl.BlockSpec((ts, D_MLP), lambda b, s: (row(b, s), OFF_U // D_MLP)),
            pl.BlockSpec((ts, D_MLP), lambda b, s: (row(b, s), OFF_VM // D_MLP)),
            pl.BlockSpec((ts, D_MLP), lambda b, s: (row(b, s), OFF_GM // D_MLP)),
            pl.BlockSpec((None, G_MLP, GMLP_CHUNK, GMLP_CHUNK),
                         lambda b, s: (layer, 0, 0, 0)),
            pl.BlockSpec((ts, G_MLP), lambda b, s: (0, 0)),
            pl.BlockSpec((1, 1, D_MLP), lambda b, s: (layer, 0, 0)),
            pl.BlockSpec((1, 1, D_MLP), lambda b, s: (layer, 0, 0)),
        ] + ([_ANY] if vn_dest is not None else []),
        out_specs=[
            pl.BlockSpec((ts, D_MLP), lambda b, s: (b * ns + s, 0)),
            pl.BlockSpec((None, ts, D_MLP), lambda b, s: (layer, b * ns + s, 0)),
        ],
        out_shape=[jax.ShapeDtypeStruct((t, D_MLP), BF16),
                   jax.ShapeDtypeStruct((DEPTH, t, D_MLP), F32)],
        input_output_aliases=aliases,
        scratch_shapes=[pltpu.VMEM((G_MLP, ts, ts), BF16)],
        compiler_params=_params("arbitrary", "arbitrary"),
        name="gmlp",
    )(*args)


def _merge_out_kernel(r_ref, m_ref, ar_ref, am_ref, x_ref, wr_ref, wm_ref,
                      wo_ref, g_ref, *rest, final_norm):
    outs = rest[-1:] if final_norm else rest[-2:]
    rb = jnp.dot(r_ref[...], wr_ref[...], preferred_element_type=F32)
    mb = jnp.dot(m_ref[...], wm_ref[...], preferred_element_type=F32)
    merged = ar_ref[...].astype(F32) * rb + am_ref[...].astype(F32) * mb
    y = x_ref[...] + jnp.dot(merged.astype(BF16), wo_ref[...],
                             preferred_element_type=F32)
    if final_norm:
        outs[0][...] = _rmsnorm(y, g_ref[0])
    else:
        outs[0][...] = y
        outs[1][...] = _rmsnorm(y, g_ref[0]).astype(BF16)


def _merge_out(r, m, z, x, x_row0, w_ret, w_mlp, w_o, g, g_layer,
               row0, dest, final_norm):
    tm = r.shape[0]
    r0 = row0 // tm
    xr0 = x_row0 // tm
    weight = lambda: _resident((D_MODEL, D_MODEL), lambda i: (0, 0))
    in_specs = [
        pl.BlockSpec((tm, RV), lambda i: (i, 0)),
        pl.BlockSpec((tm, D_MLP), lambda i: (i, 0)),
        pl.BlockSpec((tm, D_MODEL), lambda i: (r0 + i, OFF_AR // D_MODEL)),
        pl.BlockSpec((tm, D_MODEL), lambda i: (r0 + i, OFF_AM // D_MODEL)),
        pl.BlockSpec((tm, D_MODEL), lambda i: (xr0 + i, 0)),
        weight(), weight(), weight(),
        pl.BlockSpec((1, 1, D_MODEL), lambda i: (g_layer, 0, 0)),
    ]
    args = [r, m, z, z, x, w_ret, w_mlp, w_o, g]
    if final_norm:
        out_specs = pl.BlockSpec((tm, D_MODEL), lambda i: (i, 0))
        out_shape = jax.ShapeDtypeStruct((tm, D_MODEL), F32)
        aliases = {}
    else:
        out_block = pl.BlockSpec((tm, D_MODEL), lambda i: (r0 + i, 0))
        out_specs = [out_block, out_block]
        out_shape = [jax.ShapeDtypeStruct(dest[0].shape, F32),
                     jax.ShapeDtypeStruct(dest[1].shape, BF16)]
        in_specs += [_ANY, _ANY]
        args += list(dest)
        aliases = {len(args) - 2: 0, len(args) - 1: 1}
    return pl.pallas_call(
        functools.partial(_merge_out_kernel, final_norm=final_norm),
        grid=(1,),
        in_specs=in_specs,
        out_specs=out_specs,
        out_shape=out_shape,
        input_output_aliases=aliases,
        compiler_params=_params("arbitrary"),
        name="merge_out",
    )(*args)


def kernel(x_prompt, x_sample, state_ret, norm_g, w_in, ws, ws_b, ln_g, ln_b,
           w_ret_out, w_mlp_out, w_o, final_g):
    b_p, s_p, _ = x_prompt.shape
    b_s, s_s, _ = x_sample.shape
    t_p = b_p * s_p
    t_s = b_s * s_s
    t_all = t_p + t_s
    tab_p = _retention_tables(TS_MIX, CHUNK)
    tab_s = _retention_tables(s_s, s_s)
    cos_p, sin_p = _rotary_tables(s_p, 0)
    cos_s, sin_s = _rotary_tables(s_s, PAST_LEN)
    cos_rows = jnp.concatenate([jnp.tile(cos_p, (b_p, 1)), jnp.tile(cos_s, (b_s, 1))])
    sin_rows = jnp.concatenate([jnp.tile(sin_p, (b_p, 1)), jnp.tile(sin_s, (b_s, 1))])
    norm_g3 = norm_g.reshape(DEPTH, 1, D_MODEL)
    ln_g3 = ln_g.reshape(DEPTH, 1, D_MLP)
    ln_b3 = ln_b.reshape(DEPTH, 1, D_MLP)
    fg3 = final_g.reshape(1, 1, D_MODEL)
    out_weights = (w_ret_out, w_mlp_out, w_o)

    x_p, x_s, xs_row0 = x_prompt.reshape(t_p, D_MODEL), x_sample.reshape(t_s, D_MODEL), 0
    h, wb = _prenorm(x_p, norm_g3, 0, TM_OUT, t_all, 0, cast_weights=out_weights)
    h, _ = _prenorm(x_s, norm_g3, 0, t_s, t_all, t_p, dest=h)
    st_p = st_s = v_s = None
    for l in range(DEPTH):
        last = l == DEPTH - 1
        z = _in_proj(h, w_in, l, cos_rows, sin_rows)

        w_ret_b, w_mlp_b, w_o_b = wb
        merged_p, st_p, wb = _mixer(z, tab_p, ws, ws_b, ln_g3, ln_b3, w_ret_b, w_mlp_b,
                                    l, b_p, s_p, st_p, () if last else out_weights)
        r_s, st_s = _retention(z, state_ret, tab_s, l, t_p, b_s, s_s, s_s, st_s)
        m_s, v_s = _gmlp(z, ws, ws_b, ln_g3, ln_b3, l, t_p, b_s, s_s, s_s, v_s)

        g, g_layer = (fg3, 0) if last else (norm_g3, l + 1)
        out_p = _out_proj(merged_p, x_p, w_o_b, g, g_layer, t_all, last)
        out_s = _merge_out(r_s, m_s, z, x_s, xs_row0, w_ret_b, w_mlp_b, w_o_b,
                           g, g_layer, t_p, None if last else out_p, last)
        if last:
            y_p, y_s = out_p, out_s
        else:
            x_all, h = out_s
            x_p, x_s, xs_row0 = x_all, x_all, t_p
    return (y_p.reshape(b_p, s_p, D_MODEL), y_s.reshape(b_s, s_s, D_MODEL),
            st_p, st_s, v_s.reshape(DEPTH, b_s, s_s, D_MLP))
```

```python
import functools

import jax
import jax.numpy as jnp
import numpy as np
from jax import lax
from jax.experimental import pallas as pl
from jax.experimental.pallas import tpu as pltpu

D_MODEL = 2048
DEPTH = 4
PAST_LEN = 1024
CHUNK = 64
H_R = 8
DK = D_MODEL // 16
DV = D_MODEL // 8
QKW = H_R * DK
RV = H_R * DV
D_MLP = D_MODEL
G_MLP = 8
DG = D_MLP // G_MLP
GMLP_CHUNK = 128
ROPE_BASE = 10000.0
EPS = 1e-6
N_IN = QKW * 2 + RV * 2 + D_MLP * 3 + D_MODEL * 2

OFF_Q = 0
OFF_K = OFF_Q + QKW
OFF_V = OFF_K + QKW
OFF_GR = OFF_V + RV
OFF_U = OFF_GR + RV
OFF_VM = OFF_U + D_MLP
OFF_GM = OFF_VM + D_MLP
OFF_AR = OFF_GM + D_MLP
OFF_AM = OFF_AR + D_MODEL

VMEM_LIMIT = 56 * 1024 * 1024

TM_IN = 2080
RC_IN = 208
RC_ALIGN = 16
IN_UNROLL = 5
TN_IN = 1024
RC = 256
TS_MIX = 256
TM_OUT = 512

F32 = jnp.float32
BF16 = jnp.bfloat16


def _rmsnorm(x, g):
    return x * lax.rsqrt(jnp.mean(x * x, axis=-1, keepdims=True) + EPS) * g


def _params(*sem):
    return pltpu.CompilerParams(dimension_semantics=sem,
                                vmem_limit_bytes=VMEM_LIMIT)


def _resident(block_shape, index_map):
    return pl.BlockSpec(block_shape, index_map, pipeline_mode=pl.Buffered(1))


_ANY = pl.BlockSpec(memory_space=pl.ANY)


def _cast_weight_specs(weights, layer, n_steps, step_of):
    rows = D_MODEL // n_steps
    in_specs = [pl.BlockSpec((None, rows, D_MODEL),
                             lambda *ids: (layer, step_of(*ids), 0)) for _ in weights]
    out_specs = [pl.BlockSpec((rows, D_MODEL), lambda *ids: (step_of(*ids), 0))
                 for _ in weights]
    out_shape = [jax.ShapeDtypeStruct((D_MODEL, D_MODEL), BF16) for _ in weights]
    return in_specs, out_specs, out_shape


def _cast_weight_slabs(w_refs, wb_refs):
    for w_ref, wb_ref in zip(w_refs, wb_refs):
        wb_ref[...] = w_ref[...].astype(BF16)


def _prenorm_kernel(x_ref, g_ref, *rest, n_cast):
    h_ref = rest[len(rest) - n_cast - 1]
    h_ref[...] = _rmsnorm(x_ref[...], g_ref[0]).astype(BF16)
    _cast_weight_slabs(rest[:n_cast], rest[len(rest) - n_cast:])


def _prenorm(x, norm_g, layer, tm, total_rows, row0, dest=None, cast_weights=()):
    r0 = row0 // tm
    n_steps = x.shape[0] // tm
    w_in_specs, w_out_specs, w_out_shape = _cast_weight_specs(
        cast_weights, layer, n_steps, lambda i: i)
    in_specs = [
        pl.BlockSpec((tm, D_MODEL), lambda i: (i, 0)),
        pl.BlockSpec((1, 1, D_MODEL), lambda i: (layer, 0, 0)),
    ] + w_in_specs
    args = [x, norm_g] + list(cast_weights)
    aliases = {}
    if dest is not None:
        in_specs.append(_ANY)
        args.append(dest)
        aliases = {len(args) - 1: 0}
    outs = pl.pallas_call(
        functools.partial(_prenorm_kernel, n_cast=len(cast_weights)),
        grid=(n_steps,),
        in_specs=in_specs,
        out_specs=[pl.BlockSpec((tm, D_MODEL), lambda i: (r0 + i, 0))] + w_out_specs,
        out_shape=[jax.ShapeDtypeStruct((total_rows, D_MODEL), BF16)] + w_out_shape,
        input_output_aliases=aliases,
        compiler_params=_params("parallel"),
        name="prenorm",
    )(*args)
    return outs[0], tuple(outs[1:])


def _cast_weight_tile(w_ref, wb_scr):
    @pl.when(pl.program_id(1) == 0)
    def _():
        wb_scr[...] = w_ref[...].astype(BF16)


def _in_proj_qk_kernel(h_ref, w_ref, cos_ref, sin_ref, z_ref, wb_scr):
    _cast_weight_tile(w_ref, wb_scr)
    scale = jnp.where(pl.program_id(0) == 1, np.float32(DK ** -0.5), np.float32(1.0))

    def chunk(c, carry):
        rows = pl.ds(pl.multiple_of(c * RC_IN, RC_ALIGN), RC_IN)
        x = jnp.dot(h_ref[rows, :], wb_scr[...], preferred_element_type=F32)
        cos = cos_ref[rows, :]
        sin = sin_ref[rows, :]
        for h in range(TN_IN // DK):
            ks = slice(h * DK, (h + 1) * DK)
            xs = x[:, ks]
            r = xs * cos + pltpu.roll(xs, DK // 2, 1) * sin
            z_ref[rows, ks] = (r * scale).astype(BF16)
        return carry

    lax.fori_loop(0, TM_IN // RC_IN, chunk, 0, unroll=IN_UNROLL)


def _activation(act, x):
    if act == "copy":
        return x
    if act == "sigmoid":
        return jax.nn.sigmoid(x)
    if act == "silu":
        return x * jax.nn.sigmoid(x)
    if act == "gelu":
        return 0.5 * x * (1.0 + lax.erf(x * np.float32(np.sqrt(0.5))))
    raise ValueError(act)


def _in_proj_act_kernel(h_ref, w_ref, z_ref, wb_scr, *, act):
    _cast_weight_tile(w_ref, wb_scr)
    def chunk(c, carry):
        rows = pl.ds(pl.multiple_of(c * RC_IN, RC_ALIGN), RC_IN)
        x = jnp.dot(h_ref[rows, :], wb_scr[...], preferred_element_type=F32)
        z_ref[rows, :] = _activation(act, x).astype(BF16)
        return carry

    lax.fori_loop(0, TM_IN // RC_IN, chunk, 0, unroll=IN_UNROLL)


def _in_proj_qk(h, w_in, layer, cos_rows, sin_rows):
    t = h.shape[0]
    return pl.pallas_call(
        _in_proj_qk_kernel,
        grid=(2 * QKW // TN_IN, t // TM_IN),
        in_specs=[
            pl.BlockSpec((TM_IN, D_MODEL), lambda j, i: (i, 0)),
            pl.BlockSpec((None, D_MODEL, TN_IN), lambda j, i: (layer, 0, j)),
            pl.BlockSpec((TM_IN, DK), lambda j, i: (i, 0)),
            pl.BlockSpec((TM_IN, DK), lambda j, i: (i, 0)),
        ],
        out_specs=pl.BlockSpec((TM_IN, TN_IN), lambda j, i: (i, j)),
        out_shape=jax.ShapeDtypeStruct((t, 2 * QKW), BF16),
        scratch_shapes=[pltpu.VMEM((D_MODEL, TN_IN), BF16)],
        compiler_params=_params("arbitrary", "arbitrary"),
        name="in_proj_qk",
    )(h, w_in, cos_rows, sin_rows)


def _in_proj_act(h, w_in, layer, offsets, width, act):
    t = h.shape[0]
    per = width // TN_IN

    def src_tile(j):
        tile = offsets[0] // TN_IN
        for n, off in enumerate(offsets[1:], start=1):
            tile = jnp.where(j // per == n, off // TN_IN, tile)
        return tile + j % per

    return pl.pallas_call(
        functools.partial(_in_proj_act_kernel, act=act),
        grid=(len(offsets) * per, t // TM_IN),
        in_specs=[
            pl.BlockSpec((TM_IN, D_MODEL), lambda j, i: (i, 0)),
            pl.BlockSpec((None, D_MODEL, TN_IN), lambda j, i: (layer, 0, src_tile(j))),
        ],
        out_specs=pl.BlockSpec((TM_IN, TN_IN), lambda j, i: (i, j)),
        out_shape=jax.ShapeDtypeStruct((t, len(offsets) * width), BF16),
        scratch_shapes=[pltpu.VMEM((D_MODEL, TN_IN), BF16)],
        compiler_params=_params("arbitrary", "arbitrary"),
        name="in_proj_" + act,
    )(h, w_in)


def _retention_head(h, q_ref, k_ref, v_ref, gr_ref, mask_ref, qdec_ref, kdec_ref,
                    sdec_ref, s_ref, o_ref):
    ks = slice(h * DK, (h + 1) * DK)
    vs = slice(h * DV, (h + 1) * DV)
    qb = q_ref[:, ks]
    kb = k_ref[:, ks]
    vb = v_ref[:, vs]
    state = s_ref[0, h]

    scores = lax.dot_general(qb, kb, (((1,), (1,)), ((), ())),
                             preferred_element_type=F32)
    p = (scores * mask_ref[h]).astype(BF16)
    intra = jnp.dot(p, vb, preferred_element_type=F32)
    qd = (qb.astype(F32) * qdec_ref[h]).astype(BF16)
    cross = jnp.dot(qd, state.astype(BF16), preferred_element_type=F32)
    o = intra + cross

    kd = (kb.astype(F32) * kdec_ref[h]).astype(BF16)
    s_ref[0, h] = sdec_ref[h] * state + lax.dot_general(
        kd, vb, (((0,), (0,)), ((), ())), preferred_element_type=F32)

    mu = jnp.mean(o, axis=-1, keepdims=True)
    d = o - mu
    var = jnp.mean(d * d, axis=-1, keepdims=True)
    on = d * lax.rsqrt(var + EPS)
    o_ref[:, vs] = (gr_ref[:, vs].astype(F32) * on).astype(BF16)


def _build_mixing_matrix(ws_ref, w2_scr, ts, lg):
    i = lax.broadcasted_iota(jnp.int32, (lg, lg), 0)
    j = lax.broadcasted_iota(jnp.int32, (lg, lg), 1)
    keep = (i // CHUNK) >= (j // CHUNK)
    w2_scr[...] = jnp.zeros_like(w2_scr)
    for g in range(G_MLP):
        wg = jnp.where(keep, ws_ref[g, :lg, :lg], 0.0).astype(BF16)
        for c in range(ts // lg):
            w2_scr[g, c * lg:(c + 1) * lg, c * lg:(c + 1) * lg] = wg


def _gmlp_norm(vm_ref, lng_ref, lnb_ref):
    gv = vm_ref[...].astype(F32)
    mu = jnp.mean(gv, axis=-1, keepdims=True)
    d = gv - mu
    var = jnp.mean(d * d, axis=-1, keepdims=True)
    return d * lax.rsqrt(var + EPS) * lng_ref[0] + lnb_ref[0]


def _gmlp_group(g, vn, u_ref, gm_ref, bias_ref, w2_scr, m_ref):
    sl = slice(g * DG, (g + 1) * DG)
    s = jnp.dot(w2_scr[g], vn[:, sl].astype(BF16),
                preferred_element_type=F32) + bias_ref[:, g:g + 1]
    m = u_ref[:, sl].astype(F32) * s
    m_ref[:, sl] = (gm_ref[:, sl].astype(F32) * m).astype(BF16)


def _retention_tables(ts, chunk):
    log_gamma = jnp.log(1.0 - 2.0 ** (-5.0 - jnp.arange(H_R, dtype=F32)))
    idx = jnp.arange(ts, dtype=F32)
    diff = idx[:, None] - idx[None, :]
    cn = (jnp.arange(ts) // chunk)[:, None]
    cm = (jnp.arange(ts) // chunk)[None, :]
    expo = jnp.where(cn == cm, jnp.abs(diff), diff)
    mask = jnp.where((cn >= cm)[None],
                     jnp.exp(log_gamma[:, None, None] * expo[None]), 0.0)
    ones = jnp.ones((1, 1, DK), F32)
    qdec = jnp.exp(log_gamma[:, None] * (idx[None, :] + 1.0))[..., None] * ones
    kdec = jnp.exp(log_gamma[:, None] * (ts - 1.0 - idx[None, :]))[..., None] * ones
    sdec = jnp.exp(log_gamma * ts)[:, None, None] * jnp.ones((1, 1, DV), F32)
    return mask, qdec, kdec, sdec


def _rotary_tables(seq, pos0):
    half = DK // 2
    inv_freq = 1.0 / (ROPE_BASE ** (jnp.arange(half, dtype=F32) / half))
    pos = (pos0 + jnp.arange(seq)).astype(F32)
    ang = pos[:, None] * inv_freq[None, :]
    cos = jnp.cos(ang)
    sin = jnp.sin(ang)
    return (jnp.concatenate([cos, cos], axis=-1),
            jnp.concatenate([-sin, sin], axis=-1))


def _mixing_bias(ws_b, layer, ts, lg):
    return jnp.tile(jnp.transpose(ws_b[layer, :, :lg]), (ts // lg, 1))


def _mixer_kernel(q_ref, k_ref, v_ref, gr_ref, gm_ref, u_ref, vm_ref, ar_ref,
                  am_ref, mask_ref, qdec_ref, kdec_ref, sdec_ref, ws_ref,
                  bias_ref, lng_ref, lnb_ref, wr_ref, wm_ref, *rest, ts, lg, n_cast):
    w2_scr, r_scr, m_scr = rest[-3:]
    outs = rest[len(rest) - 3 - n_cast - 2:len(rest) - 3]
    merged_ref, s_ref = outs[:2]
    _cast_weight_slabs(rest[:n_cast], outs[2:])

    @pl.when((pl.program_id(0) == 0) & (pl.program_id(1) == 0))
    def _():
        _build_mixing_matrix(ws_ref, w2_scr, ts, lg)

    @pl.when(pl.program_id(1) == 0)
    def _():
        s_ref[...] = jnp.zeros_like(s_ref)

    for h in range(H_R):
        _retention_head(h, q_ref, k_ref, v_ref, gr_ref, mask_ref, qdec_ref,
                        kdec_ref, sdec_ref, s_ref, r_scr)
    rb = jnp.dot(r_scr[...], wr_ref[...], preferred_element_type=F32)
    vn = _gmlp_norm(vm_ref, lng_ref, lnb_ref)
    for g in range(G_MLP):
        _gmlp_group(g, vn, u_ref, gm_ref, bias_ref, w2_scr, m_scr)
    mb = jnp.dot(m_scr[...], wm_ref[...], preferred_element_type=F32)
    merged = ar_ref[...].astype(F32) * rb + am_ref[...].astype(F32) * mb
    merged_ref[...] = merged.astype(BF16)


def _mixer(qk, v, gates, uv, a_sig, tables, ws, ws_b, ln_g, ln_b, w_ret, w_mlp,
           layer, batch, seq, state_dest, cast_weights):
    mask, qdec, kdec, sdec = tables
    ts = TS_MIX
    lg = GMLP_CHUNK
    ns = seq // ts
    row = lambda b, s: b * ns + s
    const3 = lambda b, s: (0, 0, 0)
    bias = _mixing_bias(ws_b, layer, ts, lg)
    tile = lambda width, col: pl.BlockSpec((ts, width), lambda b, s: (row(b, s), col))
    w_in_specs, w_out_specs, w_out_shape = _cast_weight_specs(
        cast_weights, layer + 1, batch * ns, row)
    args = [qk, qk, v, gates, gates, uv, uv, a_sig, a_sig, mask, qdec, kdec, sdec,
            ws, bias, ln_g, ln_b, w_ret, w_mlp] + list(cast_weights)
    aliases = {}
    if state_dest is not None:
        args.append(state_dest)
        aliases = {len(args) - 1: 1}
    outs = pl.pallas_call(
        functools.partial(_mixer_kernel, ts=ts, lg=lg, n_cast=len(cast_weights)),
        grid=(batch, ns),
        in_specs=[
            tile(QKW, 0), tile(QKW, 1),
            tile(RV, 0),
            tile(RV, 0), tile(D_MLP, 1),
            tile(D_MLP, 0), tile(D_MLP, 1),
            tile(D_MODEL, 0), tile(D_MODEL, 1),
            _resident((H_R, ts, ts), const3),
            _resident((H_R, ts, DK), const3),
            _resident((H_R, ts, DK), const3),
            _resident((H_R, 1, DV), const3),
            _resident((None, G_MLP, GMLP_CHUNK, GMLP_CHUNK),
                      lambda b, s: (layer, 0, 0, 0)),
            _resident((ts, G_MLP), lambda b, s: (0, 0)),
            _resident((1, 1, D_MLP), lambda b, s: (layer, 0, 0)),
            _resident((1, 1, D_MLP), lambda b, s: (layer, 0, 0)),
            _resident((RV, D_MODEL), lambda b, s: (0, 0)),
            _resident((D_MLP, D_MODEL), lambda b, s: (0, 0)),
        ] + w_in_specs + ([_ANY] if state_dest is not None else []),
        out_specs=[
            pl.BlockSpec((ts, D_MODEL), lambda b, s: (row(b, s), 0)),
            pl.BlockSpec((None, 1, H_R, DK, DV), lambda b, s: (layer, b, 0, 0, 0)),
        ] + w_out_specs,
        out_shape=[
            jax.ShapeDtypeStruct((batch * seq, D_MODEL), BF16),
            jax.ShapeDtypeStruct((DEPTH, batch, H_R, DK, DV), F32),
        ] + w_out_shape,
        input_output_aliases=aliases,
        scratch_shapes=[pltpu.VMEM((G_MLP, ts, ts), BF16),
                        pltpu.VMEM((ts, RV), BF16),
                        pltpu.VMEM((ts, D_MLP), BF16)],
        compiler_params=_params("arbitrary", "arbitrary"),
        name="mixer",
    )(*args)
    return outs[0], outs[1], tuple(outs[2:])


def _out_proj_kernel(mg_ref, x_ref, wo_ref, g_ref, *outs, final_norm):
    for c in range(TM_OUT // RC):
        rows = slice(c * RC, (c + 1) * RC)
        y = x_ref[rows, :] + jnp.dot(mg_ref[rows, :], wo_ref[...],
                                     preferred_element_type=F32)
        if final_norm:
            outs[0][rows, :] = _rmsnorm(y, g_ref[0])
        else:
            outs[0][rows, :] = y
            outs[1][rows, :] = _rmsnorm(y, g_ref[0]).astype(BF16)


def _out_proj(merged, x, w_o, g, g_layer, total_rows, final_norm):
    rows = merged.shape[0]
    tm = TM_OUT
    out_block = pl.BlockSpec((tm, D_MODEL), lambda i: (i, 0))
    if final_norm:
        out_specs = out_block
        out_shape = jax.ShapeDtypeStruct((rows, D_MODEL), F32)
    else:
        out_specs = [out_block, out_block]
        out_shape = [jax.ShapeDtypeStruct((total_rows, D_MODEL), F32),
                     jax.ShapeDtypeStruct((total_rows, D_MODEL), BF16)]
    return pl.pallas_call(
        functools.partial(_out_proj_kernel, final_norm=final_norm),
        grid=(rows // tm,),
        in_specs=[
            pl.BlockSpec((tm, D_MODEL), lambda i: (i, 0)),
            pl.BlockSpec((tm, D_MODEL), lambda i: (i, 0)),
            _resident((D_MODEL, D_MODEL), lambda i: (0, 0)),
            pl.BlockSpec((1, 1, D_MODEL), lambda i: (g_layer, 0, 0)),
        ],
        out_specs=out_specs,
        out_shape=out_shape,
        compiler_params=_params("parallel"),
        name="out_proj",
    )(merged, x, w_o, g)


def _retention_kernel(q_ref, k_ref, v_ref, gr_ref, mask_ref, qdec_ref, kdec_ref,
                      sdec_ref, s0_ref, *rest):
    o_ref, s_ref = rest[-2:]

    @pl.when(pl.program_id(1) == 0)
    def _():
        s_ref[...] = s0_ref[...]

    for h in range(H_R):
        _retention_head(h, q_ref, k_ref, v_ref, gr_ref, mask_ref, qdec_ref,
                        kdec_ref, sdec_ref, s_ref, o_ref)


def _retention(qk, v, gates, s0_all, tables, layer, row0, batch, seq, ts, state_dest):
    mask, qdec, kdec, sdec = tables
    ns = seq // ts
    r0 = row0 // ts
    row = lambda b, s: r0 + b * ns + s
    const3 = lambda b, s: (0, 0, 0)
    state_block = pl.BlockSpec((None, 1, H_R, DK, DV), lambda b, s: (layer, b, 0, 0, 0))
    args = [qk, qk, v, gates, mask, qdec, kdec, sdec, s0_all]
    aliases = {}
    if state_dest is not None:
        args.append(state_dest)
        aliases = {len(args) - 1: 1}
    return pl.pallas_call(
        _retention_kernel,
        grid=(batch, ns),
        in_specs=[
            pl.BlockSpec((ts, QKW), lambda b, s: (row(b, s), 0)),
            pl.BlockSpec((ts, QKW), lambda b, s: (row(b, s), 1)),
            pl.BlockSpec((ts, RV), lambda b, s: (row(b, s), 0)),
            pl.BlockSpec((ts, RV), lambda b, s: (row(b, s), 0)),
            pl.BlockSpec((H_R, ts, ts), const3),
            pl.BlockSpec((H_R, ts, DK), const3),
            pl.BlockSpec((H_R, ts, DK), const3),
            pl.BlockSpec((H_R, 1, DV), const3),
            state_block,
        ] + ([_ANY] if state_dest is not None else []),
        out_specs=[
            pl.BlockSpec((ts, RV), lambda b, s: (b * ns + s, 0)),
            state_block,
        ],
        out_shape=[
            jax.ShapeDtypeStruct((batch * seq, RV), BF16),
            jax.ShapeDtypeStruct(s0_all.shape, F32),
        ],
        input_output_aliases=aliases,
        compiler_params=_params("parallel", "arbitrary"),
        name="retention",
    )(*args)


def _gmlp_kernel(u_ref, vm_ref, gm_ref, ws_ref, bias_ref, lng_ref, lnb_ref,
                 *rest, ts, lg):
    m_ref, vn_ref, w2_scr = rest[-3:]

    @pl.when((pl.program_id(0) == 0) & (pl.program_id(1) == 0))
    def _():
        _build_mixing_matrix(ws_ref, w2_scr, ts, lg)

    vn = _gmlp_norm(vm_ref, lng_ref, lnb_ref)
    vn_ref[...] = vn
    for g in range(G_MLP):
        _gmlp_group(g, vn, u_ref, gm_ref, bias_ref, w2_scr, m_ref)


def _gmlp(uv, gates, ws, ws_b, ln_g, ln_b, layer, row0, batch, seq, ts, vn_dest):
    lg = min(GMLP_CHUNK, seq)
    ns = seq // ts
    t = batch * seq
    r0 = row0 // ts
    row = lambda b, s: r0 + b * ns + s
    args = [uv, uv, gates, ws, _mixing_bias(ws_b, layer, ts, lg), ln_g, ln_b]
    aliases = {}
    if vn_dest is not None:
        args.append(vn_dest)
        aliases = {len(args) - 1: 1}
    return pl.pallas_call(
        functools.partial(_gmlp_kernel, ts=ts, lg=lg),
        grid=(batch, ns),
        in_specs=[
            pl.BlockSpec((ts, D_MLP), lambda b, s: (row(b, s), 0)),
            pl.BlockSpec((ts, D_MLP), lambda b, s: (row(b, s), 1)),
            pl.BlockSpec((ts, D_MLP), lambda b, s: (row(b, s), 1)),
            pl.BlockSpec((None, G_MLP, GMLP_CHUNK, GMLP_CHUNK),
                         lambda b, s: (layer, 0, 0, 0)),
            pl.BlockSpec((ts, G_MLP), lambda b, s: (0, 0)),
            pl.BlockSpec((1, 1, D_MLP), lambda b, s: (layer, 0, 0)),
            pl.BlockSpec((1, 1, D_MLP), lambda b, s: (layer, 0, 0)),
        ] + ([_ANY] if vn_dest is not None else []),
        out_specs=[
            pl.BlockSpec((ts, D_MLP), lambda b, s: (b * ns + s, 0)),
            pl.BlockSpec((None, ts, D_MLP), lambda b, s: (layer, b * ns + s, 0)),
        ],
        out_shape=[jax.ShapeDtypeStruct((t, D_MLP), BF16),
                   jax.ShapeDtypeStruct((DEPTH, t, D_MLP), F32)],
        input_output_aliases=aliases,
        scratch_shapes=[pltpu.VMEM((G_MLP, ts, ts), BF16)],
        compiler_params=_params("arbitrary", "arbitrary"),
        name="gmlp",
    )(*args)


def _merge_out_kernel(r_ref, m_ref, ar_ref, am_ref, x_ref, wr_ref, wm_ref,
                      wo_ref, g_ref, *rest, final_norm):
    outs = rest[-1:] if final_norm else rest[-2:]
    rb = jnp.dot(r_ref[...], wr_ref[...], preferred_element_type=F32)
    mb = jnp.dot(m_ref[...], wm_ref[...], preferred_element_type=F32)
    merged = ar_ref[...].astype(F32) * rb + am_ref[...].astype(F32) * mb
    y = x_ref[...] + jnp.dot(merged.astype(BF16), wo_ref[...],
                             preferred_element_type=F32)
    if final_norm:
        outs[0][...] = _rmsnorm(y, g_ref[0])
    else:
        outs[0][...] = y
        outs[1][...] = _rmsnorm(y, g_ref[0]).astype(BF16)


def _merge_out(r, m, a_sig, x, x_row0, w_ret, w_mlp, w_o, g, g_layer,
               row0, dest, final_norm):
    tm = r.shape[0]
    r0 = row0 // tm
    xr0 = x_row0 // tm
    weight = lambda: _resident((D_MODEL, D_MODEL), lambda i: (0, 0))
    in_specs = [
        pl.BlockSpec((tm, RV), lambda i: (i, 0)),
        pl.BlockSpec((tm, D_MLP), lambda i: (i, 0)),
        pl.BlockSpec((tm, D_MODEL), lambda i: (r0 + i, 0)),
        pl.BlockSpec((tm, D_MODEL), lambda i: (r0 + i, 1)),
        pl.BlockSpec((tm, D_MODEL), lambda i: (xr0 + i, 0)),
        weight(), weight(), weight(),
        pl.BlockSpec((1, 1, D_MODEL), lambda i: (g_layer, 0, 0)),
    ]
    args = [r, m, a_sig, a_sig, x, w_ret, w_mlp, w_o, g]
    if final_norm:
        out_specs = pl.BlockSpec((tm, D_MODEL), lambda i: (i, 0))
        out_shape = jax.ShapeDtypeStruct((tm, D_MODEL), F32)
        aliases = {}
    else:
        out_block = pl.BlockSpec((tm, D_MODEL), lambda i: (r0 + i, 0))
        out_specs = [out_block, out_block]
        out_shape = [jax.ShapeDtypeStruct(dest[0].shape, F32),
                     jax.ShapeDtypeStruct(dest[1].shape, BF16)]
        in_specs += [_ANY, _ANY]
        args += list(dest)
        aliases = {len(args) - 2: 0, len(args) - 1: 1}
    return pl.pallas_call(
        functools.partial(_merge_out_kernel, final_norm=final_norm),
        grid=(1,),
        in_specs=in_specs,
        out_specs=out_specs,
        out_shape=out_shape,
        input_output_aliases=aliases,
        compiler_params=_params("arbitrary"),
        name="merge_out",
    )(*args)


def kernel(x_prompt, x_sample, state_ret, norm_g, w_in, ws, ws_b, ln_g, ln_b,
           w_ret_out, w_mlp_out, w_o, final_g):
    b_p, s_p, _ = x_prompt.shape
    b_s, s_s, _ = x_sample.shape
    t_p = b_p * s_p
    t_s = b_s * s_s
    t_all = t_p + t_s
    tab_p = _retention_tables(TS_MIX, CHUNK)
    tab_s = _retention_tables(s_s, s_s)
    cos_p, sin_p = _rotary_tables(s_p, 0)
    cos_s, sin_s = _rotary_tables(s_s, PAST_LEN)
    cos_rows = jnp.concatenate([jnp.tile(cos_p, (b_p, 1)), jnp.tile(cos_s, (b_s, 1))])
    sin_rows = jnp.concatenate([jnp.tile(sin_p, (b_p, 1)), jnp.tile(sin_s, (b_s, 1))])
    norm_g3 = norm_g.reshape(DEPTH, 1, D_MODEL)
    ln_g3 = ln_g.reshape(DEPTH, 1, D_MLP)
    ln_b3 = ln_b.reshape(DEPTH, 1, D_MLP)
    fg3 = final_g.reshape(1, 1, D_MODEL)
    out_weights = (w_ret_out, w_mlp_out, w_o)

    x_p, x_s, xs_row0 = x_prompt.reshape(t_p, D_MODEL), x_sample.reshape(t_s, D_MODEL), 0
    h, wb = _prenorm(x_p, norm_g3, 0, TM_OUT, t_all, 0, cast_weights=out_weights)
    h, _ = _prenorm(x_s, norm_g3, 0, t_s, t_all, t_p, dest=h)
    st_p = st_s = v_s = None
    for l in range(DEPTH):
        last = l == DEPTH - 1
        qk = _in_proj_qk(h, w_in, l, cos_rows, sin_rows)
        v = _in_proj_act(h, w_in, l, (OFF_V,), RV, "copy")
        gates = _in_proj_act(h, w_in, l, (OFF_GR, OFF_GM), RV, "silu")
        uv = _in_proj_act(h, w_in, l, (OFF_U, OFF_VM), D_MLP, "gelu")
        a_sig = _in_proj_act(h, w_in, l, (OFF_AR, OFF_AM), D_MODEL, "sigmoid")

        w_ret_b, w_mlp_b, w_o_b = wb
        merged_p, st_p, wb = _mixer(qk, v, gates, uv, a_sig, tab_p, ws, ws_b, ln_g3,
                                    ln_b3, w_ret_b, w_mlp_b, l, b_p, s_p, st_p,
                                    () if last else out_weights)
        r_s, st_s = _retention(qk, v, gates, state_ret, tab_s, l, t_p, b_s, s_s, s_s,
                               st_s)
        m_s, v_s = _gmlp(uv, gates, ws, ws_b, ln_g3, ln_b3, l, t_p, b_s, s_s, s_s, v_s)

        g, g_layer = (fg3, 0) if last else (norm_g3, l + 1)
        out_p = _out_proj(merged_p, x_p, w_o_b, g, g_layer, t_all, last)
        out_s = _merge_out(r_s, m_s, a_sig, x_s, xs_row0, w_ret_b, w_mlp_b, w_o_b,
                           g, g_layer, t_p, None if last else out_p, last)
        if last:
            y_p, y_s = out_p, out_s
        else:
            x_all, h = out_s
            x_p, x_s, xs_row0 = x_all, x_all, t_p
    return (y_p.reshape(b_p, s_p, D_MODEL), y_s.reshape(b_s, s_s, D_MODEL),
            st_p, st_s, v_s.reshape(DEPTH, b_s, s_s, D_MLP))
```

```python
import functools

import jax
import jax.numpy as jnp
import numpy as np
from jax import lax
from jax.experimental import pallas as pl
from jax.experimental.pallas import tpu as pltpu

D_MODEL = 2048
DEPTH = 4
PAST_LEN = 1024
CHUNK = 64
H_R = 8
DK = D_MODEL // 16
DV = D_MODEL // 8
QKW = H_R * DK
RV = H_R * DV
D_MLP = D_MODEL
G_MLP = 8
DG = D_MLP // G_MLP
GMLP_CHUNK = 128
ROPE_BASE = 10000.0
EPS = 1e-6
N_IN = QKW * 2 + RV * 2 + D_MLP * 3 + D_MODEL * 2

OFF_Q = 0
OFF_K = OFF_Q + QKW
OFF_V = OFF_K + QKW
OFF_GR = OFF_V + RV
OFF_U = OFF_GR + RV
OFF_VM = OFF_U + D_MLP
OFF_GM = OFF_VM + D_MLP
OFF_AR = OFF_GM + D_MLP
OFF_AM = OFF_AR + D_MODEL

VMEM_LIMIT = 56 * 1024 * 1024

W_RET, W_MLP, W_O = 0, 1, 2

TM_IN = 2080
RC_IN = 208
TN_IN = 1024
RC = 256
TS_MIX = 256
TM_OUT = 512
SAMPLE_SEQS_PER_STEP = 4

F32 = jnp.float32
BF16 = jnp.bfloat16


def _rmsnorm(x, g):
    return x * lax.rsqrt(jnp.mean(x * x, axis=-1, keepdims=True) + EPS) * g


def _params(*sem):
    return pltpu.CompilerParams(dimension_semantics=sem,
                                vmem_limit_bytes=VMEM_LIMIT)


def _resident(block_shape, index_map):
    return pl.BlockSpec(block_shape, index_map, pipeline_mode=pl.Buffered(1))


_ANY = pl.BlockSpec(memory_space=pl.ANY)


def _cast_weight_specs(weights, layer, n_steps, step_of):
    if not weights:
        return [], [], []
    rows = D_MODEL // n_steps
    in_specs = [pl.BlockSpec((None, rows, D_MODEL),
                             lambda *ids: (layer, step_of(*ids), 0)) for _ in weights]
    out_specs = [pl.BlockSpec((len(weights), rows, D_MODEL),
                              lambda *ids: (0, step_of(*ids), 0))]
    out_shape = [jax.ShapeDtypeStruct((len(weights), D_MODEL, D_MODEL), BF16)]
    return in_specs, out_specs, out_shape


def _cast_weight_slabs(w_refs, wb_ref):
    for k, w_ref in enumerate(w_refs):
        wb_ref[k] = w_ref[...].astype(BF16)


def _prenorm_kernel(x_ref, g_ref, *rest, n_cast):
    has_cast = n_cast > 0
    h_ref = rest[len(rest) - has_cast - 1]
    h_ref[...] = _rmsnorm(x_ref[...], g_ref[0]).astype(BF16)
    if has_cast:
        _cast_weight_slabs(rest[:n_cast], rest[-1])


def _prenorm(x, norm_g, layer, tm, total_rows, row0, dest=None, cast_weights=()):
    r0 = row0 // tm
    n_steps = x.shape[0] // tm
    w_in_specs, w_out_specs, w_out_shape = _cast_weight_specs(
        cast_weights, layer, n_steps, lambda i: i)
    in_specs = [
        pl.BlockSpec((tm, D_MODEL), lambda i: (i, 0)),
        pl.BlockSpec((1, 1, D_MODEL), lambda i: (layer, 0, 0)),
    ] + w_in_specs
    args = [x, norm_g] + list(cast_weights)
    aliases = {}
    if dest is not None:
        in_specs.append(_ANY)
        args.append(dest)
        aliases = {len(args) - 1: 0}
    outs = pl.pallas_call(
        functools.partial(_prenorm_kernel, n_cast=len(cast_weights)),
        grid=(n_steps,),
        in_specs=in_specs,
        out_specs=[pl.BlockSpec((tm, D_MODEL), lambda i: (r0 + i, 0))] + w_out_specs,
        out_shape=[jax.ShapeDtypeStruct((total_rows, D_MODEL), BF16)] + w_out_shape,
        input_output_aliases=aliases,
        compiler_params=_params("parallel"),
        name="prenorm",
    )(*args)
    return outs[0], (outs[1] if cast_weights else None)


def _cast_weight_tile(w_ref, wb_scr):
    @pl.when(pl.program_id(1) == 0)
    def _():
        wb_scr[...] = w_ref[...].astype(BF16)


def _in_proj_qk_kernel(h_ref, w_ref, cos_ref, sin_ref, z_ref, wb_scr):
    _cast_weight_tile(w_ref, wb_scr)
    scale = jnp.where(pl.program_id(0) == 1, np.float32(DK ** -0.5), np.float32(1.0))
    for c in range(TM_IN // RC_IN):
        rows = slice(c * RC_IN, (c + 1) * RC_IN)
        x = jnp.dot(h_ref[rows, :], wb_scr[...], preferred_element_type=F32)
        cos = cos_ref[rows, :]
        sin = sin_ref[rows, :]
        for h in range(TN_IN // DK):
            ks = slice(h * DK, (h + 1) * DK)
            xs = x[:, ks]
            r = xs * cos + pltpu.roll(xs, DK // 2, 1) * sin
            z_ref[rows, ks] = (r * scale).astype(BF16)


def _activation(act, x):
    if act == "copy":
        return x
    if act == "sigmoid":
        return jax.nn.sigmoid(x)
    if act == "silu":
        return x * jax.nn.sigmoid(x)
    if act == "gelu":
        return 0.5 * x * (1.0 + lax.erf(x * np.float32(np.sqrt(0.5))))
    raise ValueError(act)


def _in_proj_act_kernel(h_ref, w_ref, z_ref, wb_scr, *, act):
    _cast_weight_tile(w_ref, wb_scr)
    for c in range(TM_IN // RC_IN):
        rows = slice(c * RC_IN, (c + 1) * RC_IN)
        x = jnp.dot(h_ref[rows, :], wb_scr[...], preferred_element_type=F32)
        z_ref[rows, :] = _activation(act, x).astype(BF16)


def _in_proj_qk(h, w_in, layer, cos_rows, sin_rows):
    t = h.shape[0]
    return pl.pallas_call(
        _in_proj_qk_kernel,
        grid=(2 * QKW // TN_IN, t // TM_IN),
        in_specs=[
            pl.BlockSpec((TM_IN, D_MODEL), lambda j, i: (i, 0)),
            pl.BlockSpec((None, D_MODEL, TN_IN), lambda j, i: (layer, 0, j)),
            pl.BlockSpec((TM_IN, DK), lambda j, i: (i, 0)),
            pl.BlockSpec((TM_IN, DK), lambda j, i: (i, 0)),
        ],
        out_specs=pl.BlockSpec((TM_IN, TN_IN), lambda j, i: (i, j)),
        out_shape=jax.ShapeDtypeStruct((t, 2 * QKW), BF16),
        scratch_shapes=[pltpu.VMEM((D_MODEL, TN_IN), BF16)],
        compiler_params=_params("arbitrary", "arbitrary"),
        name="in_proj_qk",
    )(h, w_in, cos_rows, sin_rows)


def _in_proj_act(h, w_in, layer, offsets, width, act):
    t = h.shape[0]
    per = width // TN_IN

    def src_tile(j):
        tile = offsets[0] // TN_IN
        for n, off in enumerate(offsets[1:], start=1):
            tile = jnp.where(j // per == n, off // TN_IN, tile)
        return tile + j % per

    return pl.pallas_call(
        functools.partial(_in_proj_act_kernel, act=act),
        grid=(len(offsets) * per, t // TM_IN),
        in_specs=[
            pl.BlockSpec((TM_IN, D_MODEL), lambda j, i: (i, 0)),
            pl.BlockSpec((None, D_MODEL, TN_IN), lambda j, i: (layer, 0, src_tile(j))),
        ],
        out_specs=pl.BlockSpec((TM_IN, TN_IN), lambda j, i: (i, j)),
        out_shape=jax.ShapeDtypeStruct((t, len(offsets) * width), BF16),
        scratch_shapes=[pltpu.VMEM((D_MODEL, TN_IN), BF16)],
        compiler_params=_params("arbitrary", "arbitrary"),
        name="in_proj_" + act,
    )(h, w_in)


def _retention_head(h, q_ref, k_ref, v_ref, gr_ref, mask_ref, qdec_ref, kdec_ref,
                    sdec_ref, s_ref, o_ref):
    ks = slice(h * DK, (h + 1) * DK)
    vs = slice(h * DV, (h + 1) * DV)
    qb = q_ref[:, ks]
    kb = k_ref[:, ks]
    vb = v_ref[:, vs]
    state = s_ref[0, h]

    scores = lax.dot_general(qb, kb, (((1,), (1,)), ((), ())),
                             preferred_element_type=F32)
    p = (scores * mask_ref[h]).astype(BF16)
    intra = jnp.dot(p, vb, preferred_element_type=F32)
    qd = (qb.astype(F32) * qdec_ref[h]).astype(BF16)
    cross = jnp.dot(qd, state.astype(BF16), preferred_element_type=F32)
    o = intra + cross

    kd = (kb.astype(F32) * kdec_ref[h]).astype(BF16)
    s_ref[0, h] = sdec_ref[h] * state + lax.dot_general(
        kd, vb, (((0,), (0,)), ((), ())), preferred_element_type=F32)

    mu = jnp.mean(o, axis=-1, keepdims=True)
    d = o - mu
    var = jnp.mean(d * d, axis=-1, keepdims=True)
    on = d * lax.rsqrt(var + EPS)
    o_ref[:, vs] = (gr_ref[:, vs].astype(F32) * on).astype(BF16)


def _build_mixing_matrix(ws_ref, w2_scr, ts, lg):
    i = lax.broadcasted_iota(jnp.int32, (lg, lg), 0)
    j = lax.broadcasted_iota(jnp.int32, (lg, lg), 1)
    keep = (i // CHUNK) >= (j // CHUNK)
    w2_scr[...] = jnp.zeros_like(w2_scr)
    for g in range(G_MLP):
        wg = jnp.where(keep, ws_ref[g, :lg, :lg], 0.0).astype(BF16)
        for c in range(ts // lg):
            w2_scr[g, c * lg:(c + 1) * lg, c * lg:(c + 1) * lg] = wg


def _gmlp_norm(vm_ref, lng_ref, lnb_ref):
    gv = vm_ref[...].astype(F32)
    mu = jnp.mean(gv, axis=-1, keepdims=True)
    d = gv - mu
    var = jnp.mean(d * d, axis=-1, keepdims=True)
    return d * lax.rsqrt(var + EPS) * lng_ref[0] + lnb_ref[0]


def _gmlp_group(g, vn, u_ref, gm_ref, bias_ref, w2_scr, m_ref):
    sl = slice(g * DG, (g + 1) * DG)
    s = jnp.dot(w2_scr[g], vn[:, sl].astype(BF16),
                preferred_element_type=F32) + bias_ref[:, g:g + 1]
    m = u_ref[:, sl].astype(F32) * s
    m_ref[:, sl] = (gm_ref[:, sl].astype(F32) * m).astype(BF16)


def _retention_tables(ts, chunk):
    log_gamma = jnp.log(1.0 - 2.0 ** (-5.0 - jnp.arange(H_R, dtype=F32)))
    idx = jnp.arange(ts, dtype=F32)
    diff = idx[:, None] - idx[None, :]
    cn = (jnp.arange(ts) // chunk)[:, None]
    cm = (jnp.arange(ts) // chunk)[None, :]
    expo = jnp.where(cn == cm, jnp.abs(diff), diff)
    mask = jnp.where((cn >= cm)[None],
                     jnp.exp(log_gamma[:, None, None] * expo[None]), 0.0)
    ones = jnp.ones((1, 1, DK), F32)
    qdec = jnp.exp(log_gamma[:, None] * (idx[None, :] + 1.0))[..., None] * ones
    kdec = jnp.exp(log_gamma[:, None] * (ts - 1.0 - idx[None, :]))[..., None] * ones
    sdec = jnp.exp(log_gamma * ts)[:, None, None] * jnp.ones((1, 1, DV), F32)
    return mask, qdec, kdec, sdec


def _rotary_tables(seq, pos0):
    half = DK // 2
    inv_freq = 1.0 / (ROPE_BASE ** (jnp.arange(half, dtype=F32) / half))
    pos = (pos0 + jnp.arange(seq)).astype(F32)
    ang = pos[:, None] * inv_freq[None, :]
    cos = jnp.cos(ang)
    sin = jnp.sin(ang)
    return (jnp.concatenate([cos, cos], axis=-1),
            jnp.concatenate([-sin, sin], axis=-1))


def _mixing_bias(ws_b, layer, ts, lg):
    return jnp.tile(jnp.transpose(ws_b[layer, :, :lg]), (ts // lg, 1))


def _mixer_kernel(q_ref, k_ref, v_ref, gr_ref, gm_ref, u_ref, vm_ref, ar_ref,
                  am_ref, mask_ref, qdec_ref, kdec_ref, sdec_ref, ws_ref,
                  bias_ref, lng_ref, lnb_ref, wr_ref, wm_ref, *rest, ts, lg, n_cast):
    w2_scr, r_scr, m_scr = rest[-3:]
    has_cast = n_cast > 0
    outs = rest[len(rest) - 3 - has_cast - 2:len(rest) - 3]
    merged_ref, s_ref = outs[:2]
    if has_cast:
        _cast_weight_slabs(rest[:n_cast], outs[2])

    @pl.when((pl.program_id(0) == 0) & (pl.program_id(1) == 0))
    def _():
        _build_mixing_matrix(ws_ref, w2_scr, ts, lg)

    @pl.when(pl.program_id(1) == 0)
    def _():
        s_ref[...] = jnp.zeros_like(s_ref)

    for h in range(H_R):
        _retention_head(h, q_ref, k_ref, v_ref, gr_ref, mask_ref, qdec_ref,
                        kdec_ref, sdec_ref, s_ref, r_scr)
    rb = jnp.dot(r_scr[...], wr_ref[...], preferred_element_type=F32)
    vn = _gmlp_norm(vm_ref, lng_ref, lnb_ref)
    for g in range(G_MLP):
        _gmlp_group(g, vn, u_ref, gm_ref, bias_ref, w2_scr, m_scr)
    mb = jnp.dot(m_scr[...], wm_ref[...], preferred_element_type=F32)
    merged = ar_ref[...].astype(F32) * rb + am_ref[...].astype(F32) * mb
    merged_ref[...] = merged.astype(BF16)


def _mixer(qk, v, gates, uv, a_sig, tables, ws, ws_b, ln_g, ln_b, wb,
           layer, batch, seq, state_dest, cast_weights):
    mask, qdec, kdec, sdec = tables
    ts = TS_MIX
    lg = GMLP_CHUNK
    ns = seq // ts
    row = lambda b, s: b * ns + s
    const3 = lambda b, s: (0, 0, 0)
    bias = _mixing_bias(ws_b, layer, ts, lg)
    tile = lambda width, col: pl.BlockSpec((ts, width), lambda b, s: (row(b, s), col))
    w_in_specs, w_out_specs, w_out_shape = _cast_weight_specs(
        cast_weights, layer + 1, batch * ns, row)
    args = [qk, qk, v, gates, gates, uv, uv, a_sig, a_sig, mask, qdec, kdec, sdec,
            ws, bias, ln_g, ln_b, wb, wb] + list(cast_weights)
    aliases = {}
    if state_dest is not None:
        args.append(state_dest)
        aliases = {len(args) - 1: 1}
    outs = pl.pallas_call(
        functools.partial(_mixer_kernel, ts=ts, lg=lg, n_cast=len(cast_weights)),
        grid=(batch, ns),
        in_specs=[
            tile(QKW, 0), tile(QKW, 1),
            tile(RV, 0),
            tile(RV, 0), tile(D_MLP, 1),
            tile(D_MLP, 0), tile(D_MLP, 1),
            tile(D_MODEL, 0), tile(D_MODEL, 1),
            _resident((H_R, ts, ts), const3),
            _resident((H_R, ts, DK), const3),
            _resident((H_R, ts, DK), const3),
            _resident((H_R, 1, DV), const3),
            _resident((None, G_MLP, GMLP_CHUNK, GMLP_CHUNK),
                      lambda b, s: (layer, 0, 0, 0)),
            _resident((ts, G_MLP), lambda b, s: (0, 0)),
            _resident((1, 1, D_MLP), lambda b, s: (layer, 0, 0)),
            _resident((1, 1, D_MLP), lambda b, s: (layer, 0, 0)),
            _resident((None, RV, D_MODEL), lambda b, s: (W_RET, 0, 0)),
            _resident((None, D_MLP, D_MODEL), lambda b, s: (W_MLP, 0, 0)),
        ] + w_in_specs + ([_ANY] if state_dest is not None else []),
        out_specs=[
            pl.BlockSpec((ts, D_MODEL), lambda b, s: (row(b, s), 0)),
            pl.BlockSpec((None, 1, H_R, DK, DV), lambda b, s: (layer, b, 0, 0, 0)),
        ] + w_out_specs,
        out_shape=[
            jax.ShapeDtypeStruct((batch * seq, D_MODEL), BF16),
            jax.ShapeDtypeStruct((DEPTH, batch, H_R, DK, DV), F32),
        ] + w_out_shape,
        input_output_aliases=aliases,
        scratch_shapes=[pltpu.VMEM((G_MLP, ts, ts), BF16),
                        pltpu.VMEM((ts, RV), BF16),
                        pltpu.VMEM((ts, D_MLP), BF16)],
        compiler_params=_params("arbitrary", "arbitrary"),
        name="mixer",
    )(*args)
    return outs[0], outs[1], (outs[2] if cast_weights else None)


def _out_proj_kernel(mg_ref, x_ref, wo_ref, g_ref, *outs, final_norm):
    for c in range(TM_OUT // RC):
        rows = slice(c * RC, (c + 1) * RC)
        y = x_ref[rows, :] + jnp.dot(mg_ref[rows, :], wo_ref[...],
                                     preferred_element_type=F32)
        if final_norm:
            outs[0][rows, :] = _rmsnorm(y, g_ref[0])
        else:
            outs[0][rows, :] = y
            outs[1][rows, :] = _rmsnorm(y, g_ref[0]).astype(BF16)


def _out_proj(merged, x, wb, g, g_layer, total_rows, final_norm):
    rows = merged.shape[0]
    tm = TM_OUT
    out_block = pl.BlockSpec((tm, D_MODEL), lambda i: (i, 0))
    if final_norm:
        out_specs = out_block
        out_shape = jax.ShapeDtypeStruct((rows, D_MODEL), F32)
    else:
        out_specs = [out_block, out_block]
        out_shape = [jax.ShapeDtypeStruct((total_rows, D_MODEL), F32),
                     jax.ShapeDtypeStruct((total_rows, D_MODEL), BF16)]
    return pl.pallas_call(
        functools.partial(_out_proj_kernel, final_norm=final_norm),
        grid=(rows // tm,),
        in_specs=[
            pl.BlockSpec((tm, D_MODEL), lambda i: (i, 0)),
            pl.BlockSpec((tm, D_MODEL), lambda i: (i, 0)),
            _resident((None, D_MODEL, D_MODEL), lambda i: (W_O, 0, 0)),
            pl.BlockSpec((1, 1, D_MODEL), lambda i: (g_layer, 0, 0)),
        ],
        out_specs=out_specs,
        out_shape=out_shape,
        compiler_params=_params("parallel"),
        name="out_proj",
    )(merged, x, wb, g)


def _retention_kernel(q_ref, k_ref, v_ref, gr_ref, mask_ref, qdec_ref, kdec_ref,
                      sdec_ref, s0_ref, *rest, nb, ts):
    o_ref, s_ref = rest[-2:]

    @pl.when(pl.program_id(1) == 0)
    def _():
        s_ref[...] = s0_ref[...]

    for bi in range(nb):
        rows = slice(bi * ts, (bi + 1) * ts)
        for h in range(H_R):
            _retention_head(h, q_ref.at[rows], k_ref.at[rows], v_ref.at[rows],
                            gr_ref.at[rows], mask_ref, qdec_ref, kdec_ref, sdec_ref,
                            s_ref.at[bi:bi + 1], o_ref.at[rows])


def _retention(qk, v, gates, s0_all, tables, layer, row0, batch, seq, state_dest, nb):
    mask, qdec, kdec, sdec = tables
    ns = 1
    ts = seq
    rows = nb * ts
    batch = batch // nb
    r0 = row0 // rows
    row = lambda b, s: r0 + b
    const3 = lambda b, s: (0, 0, 0)
    state_block = pl.BlockSpec((None, nb, H_R, DK, DV), lambda b, s: (layer, b, 0, 0, 0))
    args = [qk, qk, v, gates, mask, qdec, kdec, sdec, s0_all]
    aliases = {}
    if state_dest is not None:
        args.append(state_dest)
        aliases = {len(args) - 1: 1}
    return pl.pallas_call(
        functools.partial(_retention_kernel, nb=nb, ts=ts),
        grid=(batch, ns),
        in_specs=[
            pl.BlockSpec((rows, QKW), lambda b, s: (row(b, s), 0)),
            pl.BlockSpec((rows, QKW), lambda b, s: (row(b, s), 1)),
            pl.BlockSpec((rows, RV), lambda b, s: (row(b, s), 0)),
            pl.BlockSpec((rows, RV), lambda b, s: (row(b, s), 0)),
            pl.BlockSpec((H_R, ts, ts), const3),
            pl.BlockSpec((H_R, ts, DK), const3),
            pl.BlockSpec((H_R, ts, DK), const3),
            pl.BlockSpec((H_R, 1, DV), const3),
            state_block,
        ] + ([_ANY] if state_dest is not None else []),
        out_specs=[
            pl.BlockSpec((rows, RV), lambda b, s: (b, 0)),
            state_block,
        ],
        out_shape=[
            jax.ShapeDtypeStruct((batch * rows, RV), BF16),
            jax.ShapeDtypeStruct(s0_all.shape, F32),
        ],
        input_output_aliases=aliases,
        compiler_params=_params("parallel", "arbitrary"),
        name="retention",
    )(*args)


def _gmlp_kernel(u_ref, vm_ref, gm_ref, ws_ref, bias_ref, lng_ref, lnb_ref,
                 *rest, ts, lg):
    m_ref, vn_ref, w2_scr = rest[-3:]

    @pl.when((pl.program_id(0) == 0) & (pl.program_id(1) == 0))
    def _():
        _build_mixing_matrix(ws_ref, w2_scr, ts, lg)

    vn = _gmlp_norm(vm_ref, lng_ref, lnb_ref)
    vn_ref[...] = vn
    for g in range(G_MLP):
        _gmlp_group(g, vn, u_ref, gm_ref, bias_ref, w2_scr, m_ref)


def _gmlp(uv, gates, ws, ws_b, ln_g, ln_b, layer, row0, batch, seq, ts, vn_dest):
    lg = min(GMLP_CHUNK, seq)
    ns = seq // ts
    t = batch * seq
    r0 = row0 // ts
    row = lambda b, s: r0 + b * ns + s
    args = [uv, uv, gates, ws, _mixing_bias(ws_b, layer, ts, lg), ln_g, ln_b]
    aliases = {}
    if vn_dest is not None:
        args.append(vn_dest)
        aliases = {len(args) - 1: 1}
    return pl.pallas_call(
        functools.partial(_gmlp_kernel, ts=ts, lg=lg),
        grid=(batch, ns),
        in_specs=[
            pl.BlockSpec((ts, D_MLP), lambda b, s: (row(b, s), 0)),
            pl.BlockSpec((ts, D_MLP), lambda b, s: (row(b, s), 1)),
            pl.BlockSpec((ts, D_MLP), lambda b, s: (row(b, s), 1)),
            pl.BlockSpec((None, G_MLP, GMLP_CHUNK, GMLP_CHUNK),
                         lambda b, s: (layer, 0, 0, 0)),
            pl.BlockSpec((ts, G_MLP), lambda b, s: (0, 0)),
            pl.BlockSpec((1, 1, D_MLP), lambda b, s: (layer, 0, 0)),
            pl.BlockSpec((1, 1, D_MLP), lambda b, s: (layer, 0, 0)),
        ] + ([_ANY] if vn_dest is not None else []),
        out_specs=[
            pl.BlockSpec((ts, D_MLP), lambda b, s: (b * ns + s, 0)),
            pl.BlockSpec((None, ts, D_MLP), lambda b, s: (layer, b * ns + s, 0)),
        ],
        out_shape=[jax.ShapeDtypeStruct((t, D_MLP), BF16),
                   jax.ShapeDtypeStruct((DEPTH, t, D_MLP), F32)],
        input_output_aliases=aliases,
        scratch_shapes=[pltpu.VMEM((G_MLP, ts, ts), BF16)],
        compiler_params=_params("arbitrary", "arbitrary"),
        name="gmlp",
    )(*args)


def _merge_out_kernel(r_ref, m_ref, ar_ref, am_ref, x_ref, w_ref, g_ref, *rest,
                      final_norm):
    rb_scr, mg_scr = rest[-2:]
    outs = rest[-3:-2] if final_norm else rest[-4:-2]
    step = pl.program_id(0)

    @pl.when(step == W_RET)
    def _():
        rb_scr[...] = jnp.dot(r_ref[...], w_ref[...], preferred_element_type=F32)

    @pl.when(step == W_MLP)
    def _():
        mb = jnp.dot(m_ref[...], w_ref[...], preferred_element_type=F32)
        merged = ar_ref[...].astype(F32) * rb_scr[...] + am_ref[...].astype(F32) * mb
        mg_scr[...] = merged.astype(BF16)

    @pl.when(step == W_O)
    def _():
        y = x_ref[...] + jnp.dot(mg_scr[...], w_ref[...], preferred_element_type=F32)
        if final_norm:
            outs[0][...] = _rmsnorm(y, g_ref[0])
        else:
            outs[0][...] = y
            outs[1][...] = _rmsnorm(y, g_ref[0]).astype(BF16)


def _merge_out(r, m, a_sig, x, x_row0, wb, g, g_layer, row0, dest, final_norm):
    tm = r.shape[0]
    r0 = row0 // tm
    xr0 = x_row0 // tm
    in_specs = [
        pl.BlockSpec((tm, RV), lambda i: (0, 0)),
        pl.BlockSpec((tm, D_MLP), lambda i: (0, 0)),
        pl.BlockSpec((tm, D_MODEL), lambda i: (r0, 0)),
        pl.BlockSpec((tm, D_MODEL), lambda i: (r0, 1)),
        pl.BlockSpec((tm, D_MODEL), lambda i: (xr0, 0)),
        pl.BlockSpec((None, D_MODEL, D_MODEL), lambda i: (i, 0, 0)),
        pl.BlockSpec((1, 1, D_MODEL), lambda i: (g_layer, 0, 0)),
    ]
    args = [r, m, a_sig, a_sig, x, wb, g]
    if final_norm:
        out_specs = pl.BlockSpec((tm, D_MODEL), lambda i: (0, 0))
        out_shape = jax.ShapeDtypeStruct((tm, D_MODEL), F32)
        aliases = {}
    else:
        out_block = pl.BlockSpec((tm, D_MODEL), lambda i: (r0, 0))
        out_specs = [out_block, out_block]
        out_shape = [jax.ShapeDtypeStruct(dest[0].shape, F32),
                     jax.ShapeDtypeStruct(dest[1].shape, BF16)]
        in_specs += [_ANY, _ANY]
        args += list(dest)
        aliases = {len(args) - 2: 0, len(args) - 1: 1}
    return pl.pallas_call(
        functools.partial(_merge_out_kernel, final_norm=final_norm),
        grid=(3,),
        in_specs=in_specs,
        out_specs=out_specs,
        out_shape=out_shape,
        input_output_aliases=aliases,
        scratch_shapes=[pltpu.VMEM((tm, D_MODEL), F32),
                        pltpu.VMEM((tm, D_MODEL), BF16)],
        compiler_params=_params("arbitrary"),
        name="merge_out",
    )(*args)


def kernel(x_prompt, x_sample, state_ret, norm_g, w_in, ws, ws_b, ln_g, ln_b,
           w_ret_out, w_mlp_out, w_o, final_g):
    b_p, s_p, _ = x_prompt.shape
    b_s, s_s, _ = x_sample.shape
    t_p = b_p * s_p
    t_s = b_s * s_s
    t_all = t_p + t_s
    tab_p = _retention_tables(TS_MIX, CHUNK)
    tab_s = _retention_tables(s_s, s_s)
    cos_p, sin_p = _rotary_tables(s_p, 0)
    cos_s, sin_s = _rotary_tables(s_s, PAST_LEN)
    cos_rows = jnp.concatenate([jnp.tile(cos_p, (b_p, 1)), jnp.tile(cos_s, (b_s, 1))])
    sin_rows = jnp.concatenate([jnp.tile(sin_p, (b_p, 1)), jnp.tile(sin_s, (b_s, 1))])
    norm_g3 = norm_g.reshape(DEPTH, 1, D_MODEL)
    ln_g3 = ln_g.reshape(DEPTH, 1, D_MLP)
    ln_b3 = ln_b.reshape(DEPTH, 1, D_MLP)
    fg3 = final_g.reshape(1, 1, D_MODEL)
    out_weights = (w_ret_out, w_mlp_out, w_o)

    x_p, x_s, xs_row0 = x_prompt.reshape(t_p, D_MODEL), x_sample.reshape(t_s, D_MODEL), 0
    h, wb = _prenorm(x_p, norm_g3, 0, TM_OUT, t_all, 0, cast_weights=out_weights)
    h, _ = _prenorm(x_s, norm_g3, 0, t_s, t_all, t_p, dest=h)
    st_p = st_s = v_s = None
    for l in range(DEPTH):
        last = l == DEPTH - 1
        qk = _in_proj_qk(h, w_in, l, cos_rows, sin_rows)
        v = _in_proj_act(h, w_in, l, (OFF_V,), RV, "copy")
        gates = _in_proj_act(h, w_in, l, (OFF_GR, OFF_GM), RV, "silu")
        uv = _in_proj_act(h, w_in, l, (OFF_U, OFF_VM), D_MLP, "gelu")
        a_sig = _in_proj_act(h, w_in, l, (OFF_AR, OFF_AM), D_MODEL, "sigmoid")

        merged_p, st_p, wb_next = _mixer(qk, v, gates, uv, a_sig, tab_p, ws, ws_b,
                                         ln_g3, ln_b3, wb, l, b_p, s_p, st_p,
                                         () if last else out_weights)
        r_s, st_s = _retention(qk, v, gates, state_ret, tab_s, l, t_p, b_s, s_s, st_s,
                               SAMPLE_SEQS_PER_STEP)
        m_s, v_s = _gmlp(uv, gates, ws, ws_b, ln_g3, ln_b3, l, t_p, b_s, s_s, s_s, v_s)

        g, g_layer = (fg3, 0) if last else (norm_g3, l + 1)
        out_p = _out_proj(merged_p, x_p, wb, g, g_layer, t_all, last)
        out_s = _merge_out(r_s, m_s, a_sig, x_s, xs_row0, wb, g, g_layer, t_p,
                           None if last else out_p, last)
        wb = wb_next
        if last:
            y_p, y_s = out_p, out_s
        else:
            x_all, h = out_s
            x_p, x_s, xs_row0 = x_all, x_all, t_p
    return (y_p.reshape(b_p, s_p, D_MODEL), y_s.reshape(b_s, s_s, D_MODEL),
            st_p, st_s, v_s.reshape(DEPTH, b_s, s_s, D_MLP))
```

```python
import functools

import jax
import jax.numpy as jnp
import numpy as np
from jax import lax
from jax.experimental import pallas as pl
from jax.experimental.pallas import tpu as pltpu

D_MODEL = 2048
DEPTH = 4
PAST_LEN = 1024
CHUNK = 64
H_R = 8
DK = D_MODEL // 16
DV = D_MODEL // 8
QKW = H_R * DK
RV = H_R * DV
D_MLP = D_MODEL
G_MLP = 8
DG = D_MLP // G_MLP
GMLP_CHUNK = 128
ROPE_BASE = 10000.0
EPS = 1e-6
N_IN = QKW * 2 + RV * 2 + D_MLP * 3 + D_MODEL * 2

OFF_Q = 0
OFF_K = OFF_Q + QKW
OFF_V = OFF_K + QKW
OFF_GR = OFF_V + RV
OFF_U = OFF_GR + RV
OFF_VM = OFF_U + D_MLP
OFF_GM = OFF_VM + D_MLP
OFF_AR = OFF_GM + D_MLP
OFF_AM = OFF_AR + D_MODEL

VMEM_LIMIT = 56 * 1024 * 1024

W_RET, W_MLP, W_O = 0, 1, 2

TM_IN = 2080
RC_IN = 208
TN_IN = 1024
RC = 256
TS_MIX = 256
TM_OUT = 512
SAMPLE_SEQS_PER_STEP = 8

F32 = jnp.float32
BF16 = jnp.bfloat16


def _rmsnorm(x, g):
    return x * lax.rsqrt(jnp.mean(x * x, axis=-1, keepdims=True) + EPS) * g


def _params(*sem):
    return pltpu.CompilerParams(dimension_semantics=sem,
                                vmem_limit_bytes=VMEM_LIMIT)


def _resident(block_shape, index_map):
    return pl.BlockSpec(block_shape, index_map, pipeline_mode=pl.Buffered(1))


_ANY = pl.BlockSpec(memory_space=pl.ANY)


def _cast_weight_specs(weights, layer, n_steps, step_of):
    if not weights:
        return [], [], []
    rows = D_MODEL // n_steps
    in_specs = [pl.BlockSpec((None, rows, D_MODEL),
                             lambda *ids: (layer, step_of(*ids), 0)) for _ in weights]
    out_specs = [pl.BlockSpec((len(weights), rows, D_MODEL),
                              lambda *ids: (0, step_of(*ids), 0))]
    out_shape = [jax.ShapeDtypeStruct((len(weights), D_MODEL, D_MODEL), BF16)]
    return in_specs, out_specs, out_shape


def _cast_weight_slabs(w_refs, wb_ref):
    for k, w_ref in enumerate(w_refs):
        wb_ref[k] = w_ref[...].astype(BF16)


def _prenorm_kernel(x_ref, g_ref, *rest, n_cast):
    has_cast = n_cast > 0
    h_ref = rest[len(rest) - has_cast - 1]
    h_ref[...] = _rmsnorm(x_ref[...], g_ref[0]).astype(BF16)
    if has_cast:
        _cast_weight_slabs(rest[:n_cast], rest[-1])


def _prenorm(x, norm_g, layer, tm, total_rows, row0, dest=None, cast_weights=()):
    r0 = row0 // tm
    n_steps = x.shape[0] // tm
    w_in_specs, w_out_specs, w_out_shape = _cast_weight_specs(
        cast_weights, layer, n_steps, lambda i: i)
    in_specs = [
        pl.BlockSpec((tm, D_MODEL), lambda i: (i, 0)),
        pl.BlockSpec((1, 1, D_MODEL), lambda i: (layer, 0, 0)),
    ] + w_in_specs
    args = [x, norm_g] + list(cast_weights)
    aliases = {}
    if dest is not None:
        in_specs.append(_ANY)
        args.append(dest)
        aliases = {len(args) - 1: 0}
    outs = pl.pallas_call(
        functools.partial(_prenorm_kernel, n_cast=len(cast_weights)),
        grid=(n_steps,),
        in_specs=in_specs,
        out_specs=[pl.BlockSpec((tm, D_MODEL), lambda i: (r0 + i, 0))] + w_out_specs,
        out_shape=[jax.ShapeDtypeStruct((total_rows, D_MODEL), BF16)] + w_out_shape,
        input_output_aliases=aliases,
        compiler_params=_params("parallel"),
        name="prenorm",
    )(*args)
    return outs[0], (outs[1] if cast_weights else None)


def _cast_weight_tile(w_ref, wb_scr):
    @pl.when(pl.program_id(1) == 0)
    def _():
        wb_scr[...] = w_ref[...].astype(BF16)


def _in_proj_qk_kernel(h_ref, w_ref, cos_ref, sin_ref, z_ref, wb_scr):
    _cast_weight_tile(w_ref, wb_scr)
    scale = jnp.where(pl.program_id(0) == 1, np.float32(DK ** -0.5), np.float32(1.0))
    for c in range(TM_IN // RC_IN):
        rows = slice(c * RC_IN, (c + 1) * RC_IN)
        x = jnp.dot(h_ref[rows, :], wb_scr[...], preferred_element_type=F32)
        cos = cos_ref[rows, :]
        sin = sin_ref[rows, :]
        for h in range(TN_IN // DK):
            ks = slice(h * DK, (h + 1) * DK)
            xs = x[:, ks]
            r = xs * cos + pltpu.roll(xs, DK // 2, 1) * sin
            z_ref[rows, ks] = (r * scale).astype(BF16)


def _activation(act, x):
    if act == "copy":
        return x
    if act == "sigmoid":
        return jax.nn.sigmoid(x)
    if act == "silu":
        return x * jax.nn.sigmoid(x)
    if act == "gelu":
        return 0.5 * x * (1.0 + lax.erf(x * np.float32(np.sqrt(0.5))))
    raise ValueError(act)


def _in_proj_act_kernel(h_ref, w_ref, z_ref, wb_scr, *, act):
    _cast_weight_tile(w_ref, wb_scr)
    for c in range(TM_IN // RC_IN):
        rows = slice(c * RC_IN, (c + 1) * RC_IN)
        x = jnp.dot(h_ref[rows, :], wb_scr[...], preferred_element_type=F32)
        z_ref[rows, :] = _activation(act, x).astype(BF16)


def _in_proj_qk(h, w_in, layer, cos_rows, sin_rows):
    t = h.shape[0]
    return pl.pallas_call(
        _in_proj_qk_kernel,
        grid=(2 * QKW // TN_IN, t // TM_IN),
        in_specs=[
            pl.BlockSpec((TM_IN, D_MODEL), lambda j, i: (i, 0)),
            pl.BlockSpec((None, D_MODEL, TN_IN), lambda j, i: (layer, 0, j)),
            pl.BlockSpec((TM_IN, DK), lambda j, i: (i, 0)),
            pl.BlockSpec((TM_IN, DK), lambda j, i: (i, 0)),
        ],
        out_specs=pl.BlockSpec((TM_IN, TN_IN), lambda j, i: (i, j)),
        out_shape=jax.ShapeDtypeStruct((t, 2 * QKW), BF16),
        scratch_shapes=[pltpu.VMEM((D_MODEL, TN_IN), BF16)],
        compiler_params=_params("arbitrary", "arbitrary"),
        name="in_proj_qk",
    )(h, w_in, cos_rows, sin_rows)


def _in_proj_act(h, w_in, layer, offsets, width, act):
    t = h.shape[0]
    per = width // TN_IN

    def src_tile(j):
        tile = offsets[0] // TN_IN
        for n, off in enumerate(offsets[1:], start=1):
            tile = jnp.where(j // per == n, off // TN_IN, tile)
        return tile + j % per

    return pl.pallas_call(
        functools.partial(_in_proj_act_kernel, act=act),
        grid=(len(offsets) * per, t // TM_IN),
        in_specs=[
            pl.BlockSpec((TM_IN, D_MODEL), lambda j, i: (i, 0)),
            pl.BlockSpec((None, D_MODEL, TN_IN), lambda j, i: (layer, 0, src_tile(j))),
        ],
        out_specs=pl.BlockSpec((TM_IN, TN_IN), lambda j, i: (i, j)),
        out_shape=jax.ShapeDtypeStruct((t, len(offsets) * width), BF16),
        scratch_shapes=[pltpu.VMEM((D_MODEL, TN_IN), BF16)],
        compiler_params=_params("arbitrary", "arbitrary"),
        name="in_proj_" + act,
    )(h, w_in)


def _retention_head(h, q_ref, k_ref, v_ref, gr_ref, mask_ref, qdec_ref, kdec_ref,
                    sdec_ref, s_ref, o_ref):
    ks = slice(h * DK, (h + 1) * DK)
    vs = slice(h * DV, (h + 1) * DV)
    qb = q_ref[:, ks]
    kb = k_ref[:, ks]
    vb = v_ref[:, vs]
    state = s_ref[0, h]

    scores = lax.dot_general(qb, kb, (((1,), (1,)), ((), ())),
                             preferred_element_type=F32)
    p = (scores * mask_ref[h]).astype(BF16)
    intra = jnp.dot(p, vb, preferred_element_type=F32)
    qd = (qb.astype(F32) * qdec_ref[h]).astype(BF16)
    cross = jnp.dot(qd, state.astype(BF16), preferred_element_type=F32)
    o = intra + cross

    kd = (kb.astype(F32) * kdec_ref[h]).astype(BF16)
    s_ref[0, h] = sdec_ref[h] * state + lax.dot_general(
        kd, vb, (((0,), (0,)), ((), ())), preferred_element_type=F32)

    mu = jnp.mean(o, axis=-1, keepdims=True)
    d = o - mu
    var = jnp.mean(d * d, axis=-1, keepdims=True)
    on = d * lax.rsqrt(var + EPS)
    o_ref[:, vs] = (gr_ref[:, vs].astype(F32) * on).astype(BF16)


def _build_mixing_matrix(ws_ref, w2_scr, ts, lg):
    i = lax.broadcasted_iota(jnp.int32, (lg, lg), 0)
    j = lax.broadcasted_iota(jnp.int32, (lg, lg), 1)
    keep = (i // CHUNK) >= (j // CHUNK)
    w2_scr[...] = jnp.zeros_like(w2_scr)
    for g in range(G_MLP):
        wg = jnp.where(keep, ws_ref[g, :lg, :lg], 0.0).astype(BF16)
        for c in range(ts // lg):
            w2_scr[g, c * lg:(c + 1) * lg, c * lg:(c + 1) * lg] = wg


def _gmlp_norm(vm_ref, lng_ref, lnb_ref):
    gv = vm_ref[...].astype(F32)
    mu = jnp.mean(gv, axis=-1, keepdims=True)
    d = gv - mu
    var = jnp.mean(d * d, axis=-1, keepdims=True)
    return d * lax.rsqrt(var + EPS) * lng_ref[0] + lnb_ref[0]


def _gmlp_group(g, vn, u_ref, gm_ref, bias_ref, w2_scr, m_ref):
    sl = slice(g * DG, (g + 1) * DG)
    s = jnp.dot(w2_scr[g], vn[:, sl].astype(BF16),
                preferred_element_type=F32) + bias_ref[:, g:g + 1]
    m = u_ref[:, sl].astype(F32) * s
    m_ref[:, sl] = (gm_ref[:, sl].astype(F32) * m).astype(BF16)


def _retention_tables(ts, chunk):
    log_gamma = jnp.log(1.0 - 2.0 ** (-5.0 - jnp.arange(H_R, dtype=F32)))
    idx = jnp.arange(ts, dtype=F32)
    diff = idx[:, None] - idx[None, :]
    cn = (jnp.arange(ts) // chunk)[:, None]
    cm = (jnp.arange(ts) // chunk)[None, :]
    expo = jnp.where(cn == cm, jnp.abs(diff), diff)
    mask = jnp.where((cn >= cm)[None],
                     jnp.exp(log_gamma[:, None, None] * expo[None]), 0.0)
    ones = jnp.ones((1, 1, DK), F32)
    qdec = jnp.exp(log_gamma[:, None] * (idx[None, :] + 1.0))[..., None] * ones
    kdec = jnp.exp(log_gamma[:, None] * (ts - 1.0 - idx[None, :]))[..., None] * ones
    sdec = jnp.exp(log_gamma * ts)[:, None, None] * jnp.ones((1, 1, DV), F32)
    return mask, qdec, kdec, sdec


def _rotary_tables(seq, pos0):
    half = DK // 2
    inv_freq = 1.0 / (ROPE_BASE ** (jnp.arange(half, dtype=F32) / half))
    pos = (pos0 + jnp.arange(seq)).astype(F32)
    ang = pos[:, None] * inv_freq[None, :]
    cos = jnp.cos(ang)
    sin = jnp.sin(ang)
    return (jnp.concatenate([cos, cos], axis=-1),
            jnp.concatenate([-sin, sin], axis=-1))


def _mixing_bias(ws_b, layer, ts, lg):
    return jnp.tile(jnp.transpose(ws_b[layer, :, :lg]), (ts // lg, 1))


def _mixer_kernel(q_ref, k_ref, v_ref, gr_ref, gm_ref, u_ref, vm_ref, ar_ref,
                  am_ref, mask_ref, qdec_ref, kdec_ref, sdec_ref, ws_ref,
                  bias_ref, lng_ref, lnb_ref, wr_ref, wm_ref, *rest, ts, lg, n_cast):
    w2_scr, r_scr, m_scr = rest[-3:]
    has_cast = n_cast > 0
    outs = rest[len(rest) - 3 - has_cast - 2:len(rest) - 3]
    merged_ref, s_ref = outs[:2]
    if has_cast:
        _cast_weight_slabs(rest[:n_cast], outs[2])

    @pl.when((pl.program_id(0) == 0) & (pl.program_id(1) == 0))
    def _():
        _build_mixing_matrix(ws_ref, w2_scr, ts, lg)

    @pl.when(pl.program_id(1) == 0)
    def _():
        s_ref[...] = jnp.zeros_like(s_ref)

    for h in range(H_R):
        _retention_head(h, q_ref, k_ref, v_ref, gr_ref, mask_ref, qdec_ref,
                        kdec_ref, sdec_ref, s_ref, r_scr)
    rb = jnp.dot(r_scr[...], wr_ref[...], preferred_element_type=F32)
    vn = _gmlp_norm(vm_ref, lng_ref, lnb_ref)
    for g in range(G_MLP):
        _gmlp_group(g, vn, u_ref, gm_ref, bias_ref, w2_scr, m_scr)
    mb = jnp.dot(m_scr[...], wm_ref[...], preferred_element_type=F32)
    merged = ar_ref[...].astype(F32) * rb + am_ref[...].astype(F32) * mb
    merged_ref[...] = merged.astype(BF16)


def _mixer(qk, v, gates, uv, a_sig, tables, ws, ws_b, ln_g, ln_b, wb,
           layer, batch, seq, state_dest, cast_weights):
    mask, qdec, kdec, sdec = tables
    ts = TS_MIX
    lg = GMLP_CHUNK
    ns = seq // ts
    row = lambda b, s: b * ns + s
    const3 = lambda b, s: (0, 0, 0)
    bias = _mixing_bias(ws_b, layer, ts, lg)
    tile = lambda width, col: pl.BlockSpec((ts, width), lambda b, s: (row(b, s), col))
    w_in_specs, w_out_specs, w_out_shape = _cast_weight_specs(
        cast_weights, layer + 1, batch * ns, row)
    args = [qk, qk, v, gates, gates, uv, uv, a_sig, a_sig, mask, qdec, kdec, sdec,
            ws, bias, ln_g, ln_b, wb, wb] + list(cast_weights)
    aliases = {}
    if state_dest is not None:
        args.append(state_dest)
        aliases = {len(args) - 1: 1}
    outs = pl.pallas_call(
        functools.partial(_mixer_kernel, ts=ts, lg=lg, n_cast=len(cast_weights)),
        grid=(batch, ns),
        in_specs=[
            tile(QKW, 0), tile(QKW, 1),
            tile(RV, 0),
            tile(RV, 0), tile(D_MLP, 1),
            tile(D_MLP, 0), tile(D_MLP, 1),
            tile(D_MODEL, 0), tile(D_MODEL, 1),
            _resident((H_R, ts, ts), const3),
            _resident((H_R, ts, DK), const3),
            _resident((H_R, ts, DK), const3),
            _resident((H_R, 1, DV), const3),
            _resident((None, G_MLP, GMLP_CHUNK, GMLP_CHUNK),
                      lambda b, s: (layer, 0, 0, 0)),
            _resident((ts, G_MLP), lambda b, s: (0, 0)),
            _resident((1, 1, D_MLP), lambda b, s: (layer, 0, 0)),
            _resident((1, 1, D_MLP), lambda b, s: (layer, 0, 0)),
            _resident((None, RV, D_MODEL), lambda b, s: (W_RET, 0, 0)),
            _resident((None, D_MLP, D_MODEL), lambda b, s: (W_MLP, 0, 0)),
        ] + w_in_specs + ([_ANY] if state_dest is not None else []),
        out_specs=[
            pl.BlockSpec((ts, D_MODEL), lambda b, s: (row(b, s), 0)),
            pl.BlockSpec((None, 1, H_R, DK, DV), lambda b, s: (layer, b, 0, 0, 0)),
        ] + w_out_specs,
        out_shape=[
            jax.ShapeDtypeStruct((batch * seq, D_MODEL), BF16),
            jax.ShapeDtypeStruct((DEPTH, batch, H_R, DK, DV), F32),
        ] + w_out_shape,
        input_output_aliases=aliases,
        scratch_shapes=[pltpu.VMEM((G_MLP, ts, ts), BF16),
                        pltpu.VMEM((ts, RV), BF16),
                        pltpu.VMEM((ts, D_MLP), BF16)],
        compiler_params=_params("arbitrary", "arbitrary"),
        name="mixer",
    )(*args)
    return outs[0], outs[1], (outs[2] if cast_weights else None)


def _out_proj_kernel(mg_ref, x_ref, wo_ref, g_ref, *outs, final_norm):
    for c in range(TM_OUT // RC):
        rows = slice(c * RC, (c + 1) * RC)
        y = x_ref[rows, :] + jnp.dot(mg_ref[rows, :], wo_ref[...],
                                     preferred_element_type=F32)
        if final_norm:
            outs[0][rows, :] = _rmsnorm(y, g_ref[0])
        else:
            outs[0][rows, :] = y
            outs[1][rows, :] = _rmsnorm(y, g_ref[0]).astype(BF16)


def _out_proj(merged, x, wb, g, g_layer, total_rows, final_norm):
    rows = merged.shape[0]
    tm = TM_OUT
    out_block = pl.BlockSpec((tm, D_MODEL), lambda i: (i, 0))
    if final_norm:
        out_specs = out_block
        out_shape = jax.ShapeDtypeStruct((rows, D_MODEL), F32)
    else:
        out_specs = [out_block, out_block]
        out_shape = [jax.ShapeDtypeStruct((total_rows, D_MODEL), F32),
                     jax.ShapeDtypeStruct((total_rows, D_MODEL), BF16)]
    return pl.pallas_call(
        functools.partial(_out_proj_kernel, final_norm=final_norm),
        grid=(rows // tm,),
        in_specs=[
            pl.BlockSpec((tm, D_MODEL), lambda i: (i, 0)),
            pl.BlockSpec((tm, D_MODEL), lambda i: (i, 0)),
            _resident((None, D_MODEL, D_MODEL), lambda i: (W_O, 0, 0)),
            pl.BlockSpec((1, 1, D_MODEL), lambda i: (g_layer, 0, 0)),
        ],
        out_specs=out_specs,
        out_shape=out_shape,
        compiler_params=_params("parallel"),
        name="out_proj",
    )(merged, x, wb, g)


def _retention_kernel(q_ref, k_ref, v_ref, gr_ref, mask_ref, qdec_ref, kdec_ref,
                      sdec_ref, s0_ref, *rest, nb, ts):
    o_ref, s_ref = rest[-2:]

    @pl.when(pl.program_id(1) == 0)
    def _():
        s_ref[...] = s0_ref[...]

    for bi in range(nb):
        rows = slice(bi * ts, (bi + 1) * ts)
        for h in range(H_R):
            _retention_head(h, q_ref.at[rows], k_ref.at[rows], v_ref.at[rows],
                            gr_ref.at[rows], mask_ref, qdec_ref, kdec_ref, sdec_ref,
                            s_ref.at[bi:bi + 1], o_ref.at[rows])


def _retention(qk, v, gates, s0_all, tables, layer, row0, batch, seq, state_dest, nb):
    mask, qdec, kdec, sdec = tables
    ns = 1
    ts = seq
    rows = nb * ts
    batch = batch // nb
    r0 = row0 // rows
    row = lambda b, s: r0 + b
    const3 = lambda b, s: (0, 0, 0)
    state_block = pl.BlockSpec((None, nb, H_R, DK, DV), lambda b, s: (layer, b, 0, 0, 0))
    args = [qk, qk, v, gates, mask, qdec, kdec, sdec, s0_all]
    aliases = {}
    if state_dest is not None:
        args.append(state_dest)
        aliases = {len(args) - 1: 1}
    return pl.pallas_call(
        functools.partial(_retention_kernel, nb=nb, ts=ts),
        grid=(batch, ns),
        in_specs=[
            pl.BlockSpec((rows, QKW), lambda b, s: (row(b, s), 0)),
            pl.BlockSpec((rows, QKW), lambda b, s: (row(b, s), 1)),
            pl.BlockSpec((rows, RV), lambda b, s: (row(b, s), 0)),
            pl.BlockSpec((rows, RV), lambda b, s: (row(b, s), 0)),
            pl.BlockSpec((H_R, ts, ts), const3),
            pl.BlockSpec((H_R, ts, DK), const3),
            pl.BlockSpec((H_R, ts, DK), const3),
            pl.BlockSpec((H_R, 1, DV), const3),
            state_block,
        ] + ([_ANY] if state_dest is not None else []),
        out_specs=[
            pl.BlockSpec((rows, RV), lambda b, s: (b, 0)),
            state_block,
        ],
        out_shape=[
            jax.ShapeDtypeStruct((batch * rows, RV), BF16),
            jax.ShapeDtypeStruct(s0_all.shape, F32),
        ],
        input_output_aliases=aliases,
        compiler_params=_params("parallel", "arbitrary"),
        name="retention",
    )(*args)


def _gmlp_kernel(u_ref, vm_ref, gm_ref, ws_ref, bias_ref, lng_ref, lnb_ref,
                 *rest, ts, lg):
    m_ref, vn_ref, w2_scr = rest[-3:]

    @pl.when((pl.program_id(0) == 0) & (pl.program_id(1) == 0))
    def _():
        _build_mixing_matrix(ws_ref, w2_scr, ts, lg)

    vn = _gmlp_norm(vm_ref, lng_ref, lnb_ref)
    vn_ref[...] = vn
    for g in range(G_MLP):
        _gmlp_group(g, vn, u_ref, gm_ref, bias_ref, w2_scr, m_ref)


def _gmlp(uv, gates, ws, ws_b, ln_g, ln_b, layer, row0, batch, seq, ts, vn_dest):
    lg = min(GMLP_CHUNK, seq)
    ns = seq // ts
    t = batch * seq
    r0 = row0 // ts
    row = lambda b, s: r0 + b * ns + s
    args = [uv, uv, gates, ws, _mixing_bias(ws_b, layer, ts, lg), ln_g, ln_b]
    aliases = {}
    if vn_dest is not None:
        args.append(vn_dest)
        aliases = {len(args) - 1: 1}
    return pl.pallas_call(
        functools.partial(_gmlp_kernel, ts=ts, lg=lg),
        grid=(batch, ns),
        in_specs=[
            pl.BlockSpec((ts, D_MLP), lambda b, s: (row(b, s), 0)),
            pl.BlockSpec((ts, D_MLP), lambda b, s: (row(b, s), 1)),
            pl.BlockSpec((ts, D_MLP), lambda b, s: (row(b, s), 1)),
            pl.BlockSpec((None, G_MLP, GMLP_CHUNK, GMLP_CHUNK),
                         lambda b, s: (layer, 0, 0, 0)),
            pl.BlockSpec((ts, G_MLP), lambda b, s: (0, 0)),
            pl.BlockSpec((1, 1, D_MLP), lambda b, s: (layer, 0, 0)),
            pl.BlockSpec((1, 1, D_MLP), lambda b, s: (layer, 0, 0)),
        ] + ([_ANY] if vn_dest is not None else []),
        out_specs=[
            pl.BlockSpec((ts, D_MLP), lambda b, s: (b * ns + s, 0)),
            pl.BlockSpec((None, ts, D_MLP), lambda b, s: (layer, b * ns + s, 0)),
        ],
        out_shape=[jax.ShapeDtypeStruct((t, D_MLP), BF16),
                   jax.ShapeDtypeStruct((DEPTH, t, D_MLP), F32)],
        input_output_aliases=aliases,
        scratch_shapes=[pltpu.VMEM((G_MLP, ts, ts), BF16)],
        compiler_params=_params("arbitrary", "arbitrary"),
        name="gmlp",
    )(*args)


def _merge_out_kernel(r_ref, m_ref, ar_ref, am_ref, x_ref, w_ref, g_ref, *rest,
                      final_norm):
    rb_scr, mg_scr = rest[-2:]
    outs = rest[-3:-2] if final_norm else rest[-4:-2]
    step = pl.program_id(0)

    @pl.when(step == W_RET)
    def _():
        rb_scr[...] = jnp.dot(r_ref[...], w_ref[...], preferred_element_type=F32)

    @pl.when(step == W_MLP)
    def _():
        mb = jnp.dot(m_ref[...], w_ref[...], preferred_element_type=F32)
        merged = ar_ref[...].astype(F32) * rb_scr[...] + am_ref[...].astype(F32) * mb
        mg_scr[...] = merged.astype(BF16)

    @pl.when(step == W_O)
    def _():
        y = x_ref[...] + jnp.dot(mg_scr[...], w_ref[...], preferred_element_type=F32)
        if final_norm:
            outs[0][...] = _rmsnorm(y, g_ref[0])
        else:
            outs[0][...] = y
            outs[1][...] = _rmsnorm(y, g_ref[0]).astype(BF16)


def _merge_out(r, m, a_sig, x, x_row0, wb, g, g_layer, row0, dest, final_norm):
    tm = r.shape[0]
    r0 = row0 // tm
    xr0 = x_row0 // tm
    in_specs = [
        pl.BlockSpec((tm, RV), lambda i: (0, 0)),
        pl.BlockSpec((tm, D_MLP), lambda i: (0, 0)),
        pl.BlockSpec((tm, D_MODEL), lambda i: (r0, 0)),
        pl.BlockSpec((tm, D_MODEL), lambda i: (r0, 1)),
        pl.BlockSpec((tm, D_MODEL), lambda i: (xr0, 0)),
        pl.BlockSpec((None, D_MODEL, D_MODEL), lambda i: (i, 0, 0)),
        pl.BlockSpec((1, 1, D_MODEL), lambda i: (g_layer, 0, 0)),
    ]
    args = [r, m, a_sig, a_sig, x, wb, g]
    if final_norm:
        out_specs = pl.BlockSpec((tm, D_MODEL), lambda i: (0, 0))
        out_shape = jax.ShapeDtypeStruct((tm, D_MODEL), F32)
        aliases = {}
    else:
        out_block = pl.BlockSpec((tm, D_MODEL), lambda i: (r0, 0))
        out_specs = [out_block, out_block]
        out_shape = [jax.ShapeDtypeStruct(dest[0].shape, F32),
                     jax.ShapeDtypeStruct(dest[1].shape, BF16)]
        in_specs += [_ANY, _ANY]
        args += list(dest)
        aliases = {len(args) - 2: 0, len(args) - 1: 1}
    return pl.pallas_call(
        functools.partial(_merge_out_kernel, final_norm=final_norm),
        grid=(3,),
        in_specs=in_specs,
        out_specs=out_specs,
        out_shape=out_shape,
        input_output_aliases=aliases,
        scratch_shapes=[pltpu.VMEM((tm, D_MODEL), F32),
                        pltpu.VMEM((tm, D_MODEL), BF16)],
        compiler_params=_params("arbitrary"),
        name="merge_out",
    )(*args)


def kernel(x_prompt, x_sample, state_ret, norm_g, w_in, ws, ws_b, ln_g, ln_b,
           w_ret_out, w_mlp_out, w_o, final_g):
    b_p, s_p, _ = x_prompt.shape
    b_s, s_s, _ = x_sample.shape
    t_p = b_p * s_p
    t_s = b_s * s_s
    t_all = t_p + t_s
    tab_p = _retention_tables(TS_MIX, CHUNK)
    tab_s = _retention_tables(s_s, s_s)
    cos_p, sin_p = _rotary_tables(s_p, 0)
    cos_s, sin_s = _rotary_tables(s_s, PAST_LEN)
    cos_rows = jnp.concatenate([jnp.tile(cos_p, (b_p, 1)), jnp.tile(cos_s, (b_s, 1))])
    sin_rows = jnp.concatenate([jnp.tile(sin_p, (b_p, 1)), jnp.tile(sin_s, (b_s, 1))])
    norm_g3 = norm_g.reshape(DEPTH, 1, D_MODEL)
    ln_g3 = ln_g.reshape(DEPTH, 1, D_MLP)
    ln_b3 = ln_b.reshape(DEPTH, 1, D_MLP)
    fg3 = final_g.reshape(1, 1, D_MODEL)
    out_weights = (w_ret_out, w_mlp_out, w_o)

    x_p, x_s, xs_row0 = x_prompt.reshape(t_p, D_MODEL), x_sample.reshape(t_s, D_MODEL), 0
    h, wb = _prenorm(x_p, norm_g3, 0, TM_OUT, t_all, 0, cast_weights=out_weights)
    h, _ = _prenorm(x_s, norm_g3, 0, t_s, t_all, t_p, dest=h)
    st_p = st_s = v_s = None
    for l in range(DEPTH):
        last = l == DEPTH - 1
        qk = _in_proj_qk(h, w_in, l, cos_rows, sin_rows)
        v = _in_proj_act(h, w_in, l, (OFF_V,), RV, "copy")
        gates = _in_proj_act(h, w_in, l, (OFF_GR, OFF_GM), RV, "silu")
        uv = _in_proj_act(h, w_in, l, (OFF_U, OFF_VM), D_MLP, "gelu")
        a_sig = _in_proj_act(h, w_in, l, (OFF_AR, OFF_AM), D_MODEL, "sigmoid")

        merged_p, st_p, wb_next = _mixer(qk, v, gates, uv, a_sig, tab_p, ws, ws_b,
                                         ln_g3, ln_b3, wb, l, b_p, s_p, st_p,
                                         () if last else out_weights)
        r_s, st_s = _retention(qk, v, gates, state_ret, tab_s, l, t_p, b_s, s_s, st_s,
                               SAMPLE_SEQS_PER_STEP)
        m_s, v_s = _gmlp(uv, gates, ws, ws_b, ln_g3, ln_b3, l, t_p, b_s, s_s, s_s, v_s)

        g, g_layer = (fg3, 0) if last else (norm_g3, l + 1)
        out_p = _out_proj(merged_p, x_p, wb, g, g_layer, t_all, last)
        out_s = _merge_out(r_s, m_s, a_sig, x_s, xs_row0, wb, g, g_layer, t_p,
                           None if last else out_p, last)
        wb = wb_next
        if last:
            y_p, y_s = out_p, out_s
        else:
            x_all, h = out_s
            x_p, x_s, xs_row0 = x_all, x_all, t_p
    return (y_p.reshape(b_p, s_p, D_MODEL), y_s.reshape(b_s, s_s, D_MODEL),
            st_p, st_s, v_s.reshape(DEPTH, b_s, s_s, D_MLP))
```

```python
import functools

import jax
import jax.numpy as jnp
import numpy as np
from jax import lax
from jax.experimental import pallas as pl
from jax.experimental.pallas import tpu as pltpu

D_MODEL = 2048
DEPTH = 4
PAST_LEN = 1024
CHUNK = 64
H_R = 8
DK = D_MODEL // 16
DV = D_MODEL // 8
QKW = H_R * DK
RV = H_R * DV
D_MLP = D_MODEL
G_MLP = 8
DG = D_MLP // G_MLP
GMLP_CHUNK = 128
ROPE_BASE = 10000.0
EPS = 1e-6
N_IN = QKW * 2 + RV * 2 + D_MLP * 3 + D_MODEL * 2

OFF_Q = 0
OFF_K = OFF_Q + QKW
OFF_V = OFF_K + QKW
OFF_GR = OFF_V + RV
OFF_U = OFF_GR + RV
OFF_VM = OFF_U + D_MLP
OFF_GM = OFF_VM + D_MLP
OFF_AR = OFF_GM + D_MLP
OFF_AM = OFF_AR + D_MODEL

VMEM_LIMIT = 56 * 1024 * 1024

W_RET, W_MLP, W_O = 0, 1, 2

TM_IN = 2080
RC_IN = 208
TN_IN = 1024
RC = 256
TS_MIX = 256
TM_OUT = 512
SAMPLE_SEQS_PER_STEP = 2

F32 = jnp.float32
BF16 = jnp.bfloat16


def _rmsnorm(x, g):
    return x * lax.rsqrt(jnp.mean(x * x, axis=-1, keepdims=True) + EPS) * g


def _params(*sem):
    return pltpu.CompilerParams(dimension_semantics=sem,
                                vmem_limit_bytes=VMEM_LIMIT)


def _resident(block_shape, index_map):
    return pl.BlockSpec(block_shape, index_map, pipeline_mode=pl.Buffered(1))


_ANY = pl.BlockSpec(memory_space=pl.ANY)


def _cast_weight_specs(weights, layer, n_steps, step_of):
    if not weights:
        return [], [], []
    rows = D_MODEL // n_steps
    in_specs = [pl.BlockSpec((None, rows, D_MODEL),
                             lambda *ids: (layer, step_of(*ids), 0)) for _ in weights]
    out_specs = [pl.BlockSpec((len(weights), rows, D_MODEL),
                              lambda *ids: (0, step_of(*ids), 0))]
    out_shape = [jax.ShapeDtypeStruct((len(weights), D_MODEL, D_MODEL), BF16)]
    return in_specs, out_specs, out_shape


def _cast_weight_slabs(w_refs, wb_ref):
    for k, w_ref in enumerate(w_refs):
        wb_ref[k] = w_ref[...].astype(BF16)


def _prenorm_kernel(x_ref, g_ref, *rest, n_cast):
    has_cast = n_cast > 0
    h_ref = rest[len(rest) - has_cast - 1]
    h_ref[...] = _rmsnorm(x_ref[...], g_ref[0]).astype(BF16)
    if has_cast:
        _cast_weight_slabs(rest[:n_cast], rest[-1])


def _prenorm(x, norm_g, layer, tm, total_rows, row0, dest=None, cast_weights=()):
    r0 = row0 // tm
    n_steps = x.shape[0] // tm
    w_in_specs, w_out_specs, w_out_shape = _cast_weight_specs(
        cast_weights, layer, n_steps, lambda i: i)
    in_specs = [
        pl.BlockSpec((tm, D_MODEL), lambda i: (i, 0)),
        pl.BlockSpec((1, 1, D_MODEL), lambda i: (layer, 0, 0)),
    ] + w_in_specs
    args = [x, norm_g] + list(cast_weights)
    aliases = {}
    if dest is not None:
        in_specs.append(_ANY)
        args.append(dest)
        aliases = {len(args) - 1: 0}
    outs = pl.pallas_call(
        functools.partial(_prenorm_kernel, n_cast=len(cast_weights)),
        grid=(n_steps,),
        in_specs=in_specs,
        out_specs=[pl.BlockSpec((tm, D_MODEL), lambda i: (r0 + i, 0))] + w_out_specs,
        out_shape=[jax.ShapeDtypeStruct((total_rows, D_MODEL), BF16)] + w_out_shape,
        input_output_aliases=aliases,
        compiler_params=_params("parallel"),
        name="prenorm",
    )(*args)
    return outs[0], (outs[1] if cast_weights else None)


def _cast_weight_tile(w_ref, wb_scr):
    @pl.when(pl.program_id(1) == 0)
    def _():
        wb_scr[...] = w_ref[...].astype(BF16)


def _in_proj_qk_kernel(h_ref, w_ref, cos_ref, sin_ref, z_ref, wb_scr):
    _cast_weight_tile(w_ref, wb_scr)
    scale = jnp.where(pl.program_id(0) == 1, np.float32(DK ** -0.5), np.float32(1.0))
    for c in range(TM_IN // RC_IN):
        rows = slice(c * RC_IN, (c + 1) * RC_IN)
        x = jnp.dot(h_ref[rows, :], wb_scr[...], preferred_element_type=F32)
        cos = cos_ref[rows, :]
        sin = sin_ref[rows, :]
        for h in range(TN_IN // DK):
            ks = slice(h * DK, (h + 1) * DK)
            xs = x[:, ks]
            r = xs * cos + pltpu.roll(xs, DK // 2, 1) * sin
            z_ref[rows, ks] = (r * scale).astype(BF16)


def _activation(act, x):
    if act == "copy":
        return x
    if act == "sigmoid":
        return jax.nn.sigmoid(x)
    if act == "silu":
        return x * jax.nn.sigmoid(x)
    if act == "gelu":
        return 0.5 * x * (1.0 + lax.erf(x * np.float32(np.sqrt(0.5))))
    raise ValueError(act)


def _in_proj_act_kernel(h_ref, w_ref, z_ref, wb_scr, *, act):
    _cast_weight_tile(w_ref, wb_scr)
    for c in range(TM_IN // RC_IN):
        rows = slice(c * RC_IN, (c + 1) * RC_IN)
        x = jnp.dot(h_ref[rows, :], wb_scr[...], preferred_element_type=F32)
        z_ref[rows, :] = _activation(act, x).astype(BF16)


def _in_proj_qk(h, w_in, layer, cos_rows, sin_rows):
    t = h.shape[0]
    return pl.pallas_call(
        _in_proj_qk_kernel,
        grid=(2 * QKW // TN_IN, t // TM_IN),
        in_specs=[
            pl.BlockSpec((TM_IN, D_MODEL), lambda j, i: (i, 0)),
            pl.BlockSpec((None, D_MODEL, TN_IN), lambda j, i: (layer, 0, j)),
            pl.BlockSpec((TM_IN, DK), lambda j, i: (i, 0)),
            pl.BlockSpec((TM_IN, DK), lambda j, i: (i, 0)),
        ],
        out_specs=pl.BlockSpec((TM_IN, TN_IN), lambda j, i: (i, j)),
        out_shape=jax.ShapeDtypeStruct((t, 2 * QKW), BF16),
        scratch_shapes=[pltpu.VMEM((D_MODEL, TN_IN), BF16)],
        compiler_params=_params("arbitrary", "arbitrary"),
        name="in_proj_qk",
    )(h, w_in, cos_rows, sin_rows)


def _in_proj_act(h, w_in, layer, offsets, width, act):
    t = h.shape[0]
    per = width // TN_IN

    def src_tile(j):
        tile = offsets[0] // TN_IN
        for n, off in enumerate(offsets[1:], start=1):
            tile = jnp.where(j // per == n, off // TN_IN, tile)
        return tile + j % per

    return pl.pallas_call(
        functools.partial(_in_proj_act_kernel, act=act),
        grid=(len(offsets) * per, t // TM_IN),
        in_specs=[
            pl.BlockSpec((TM_IN, D_MODEL), lambda j, i: (i, 0)),
            pl.BlockSpec((None, D_MODEL, TN_IN), lambda j, i: (layer, 0, src_tile(j))),
        ],
        out_specs=pl.BlockSpec((TM_IN, TN_IN), lambda j, i: (i, j)),
        out_shape=jax.ShapeDtypeStruct((t, len(offsets) * width), BF16),
        scratch_shapes=[pltpu.VMEM((D_MODEL, TN_IN), BF16)],
        compiler_params=_params("arbitrary", "arbitrary"),
        name="in_proj_" + act,
    )(h, w_in)


def _retention_head(h, q_ref, k_ref, v_ref, gr_ref, mask_ref, qdec_ref, kdec_ref,
                    sdec_ref, s_ref, o_ref):
    ks = slice(h * DK, (h + 1) * DK)
    vs = slice(h * DV, (h + 1) * DV)
    qb = q_ref[:, ks]
    kb = k_ref[:, ks]
    vb = v_ref[:, vs]
    state = s_ref[0, h]

    scores = lax.dot_general(qb, kb, (((1,), (1,)), ((), ())),
                             preferred_element_type=F32)
    p = (scores * mask_ref[h]).astype(BF16)
    intra = jnp.dot(p, vb, preferred_element_type=F32)
    qd = (qb.astype(F32) * qdec_ref[h]).astype(BF16)
    cross = jnp.dot(qd, state.astype(BF16), preferred_element_type=F32)
    o = intra + cross

    kd = (kb.astype(F32) * kdec_ref[h]).astype(BF16)
    s_ref[0, h] = sdec_ref[h] * state + lax.dot_general(
        kd, vb, (((0,), (0,)), ((), ())), preferred_element_type=F32)

    mu = jnp.mean(o, axis=-1, keepdims=True)
    d = o - mu
    var = jnp.mean(d * d, axis=-1, keepdims=True)
    on = d * lax.rsqrt(var + EPS)
    o_ref[:, vs] = (gr_ref[:, vs].astype(F32) * on).astype(BF16)


def _build_mixing_matrix(ws_ref, w2_scr, ts, lg):
    i = lax.broadcasted_iota(jnp.int32, (lg, lg), 0)
    j = lax.broadcasted_iota(jnp.int32, (lg, lg), 1)
    keep = (i // CHUNK) >= (j // CHUNK)
    w2_scr[...] = jnp.zeros_like(w2_scr)
    for g in range(G_MLP):
        wg = jnp.where(keep, ws_ref[g, :lg, :lg], 0.0).astype(BF16)
        for c in range(ts // lg):
            w2_scr[g, c * lg:(c + 1) * lg, c * lg:(c + 1) * lg] = wg


def _gmlp_norm(vm_ref, lng_ref, lnb_ref):
    gv = vm_ref[...].astype(F32)
    mu = jnp.mean(gv, axis=-1, keepdims=True)
    d = gv - mu
    var = jnp.mean(d * d, axis=-1, keepdims=True)
    return d * lax.rsqrt(var + EPS) * lng_ref[0] + lnb_ref[0]


def _gmlp_group(g, vn, u_ref, gm_ref, bias_ref, w2_scr, m_ref):
    sl = slice(g * DG, (g + 1) * DG)
    s = jnp.dot(w2_scr[g], vn[:, sl].astype(BF16),
                preferred_element_type=F32) + bias_ref[:, g:g + 1]
    m = u_ref[:, sl].astype(F32) * s
    m_ref[:, sl] = (gm_ref[:, sl].astype(F32) * m).astype(BF16)


def _retention_tables(ts, chunk):
    log_gamma = jnp.log(1.0 - 2.0 ** (-5.0 - jnp.arange(H_R, dtype=F32)))
    idx = jnp.arange(ts, dtype=F32)
    diff = idx[:, None] - idx[None, :]
    cn = (jnp.arange(ts) // chunk)[:, None]
    cm = (jnp.arange(ts) // chunk)[None, :]
    expo = jnp.where(cn == cm, jnp.abs(diff), diff)
    mask = jnp.where((cn >= cm)[None],
                     jnp.exp(log_gamma[:, None, None] * expo[None]), 0.0)
    ones = jnp.ones((1, 1, DK), F32)
    qdec = jnp.exp(log_gamma[:, None] * (idx[None, :] + 1.0))[..., None] * ones
    kdec = jnp.exp(log_gamma[:, None] * (ts - 1.0 - idx[None, :]))[..., None] * ones
    sdec = jnp.exp(log_gamma * ts)[:, None, None] * jnp.ones((1, 1, DV), F32)
    return mask, qdec, kdec, sdec


def _rotary_tables(seq, pos0):
    half = DK // 2
    inv_freq = 1.0 / (ROPE_BASE ** (jnp.arange(half, dtype=F32) / half))
    pos = (pos0 + jnp.arange(seq)).astype(F32)
    ang = pos[:, None] * inv_freq[None, :]
    cos = jnp.cos(ang)
    sin = jnp.sin(ang)
    return (jnp.concatenate([cos, cos], axis=-1),
            jnp.concatenate([-sin, sin], axis=-1))


def _mixing_bias(ws_b, layer, ts, lg):
    return jnp.tile(jnp.transpose(ws_b[layer, :, :lg]), (ts // lg, 1))


def _mixer_kernel(q_ref, k_ref, v_ref, gr_ref, gm_ref, u_ref, vm_ref, ar_ref,
                  am_ref, mask_ref, qdec_ref, kdec_ref, sdec_ref, ws_ref,
                  bias_ref, lng_ref, lnb_ref, wr_ref, wm_ref, *rest, ts, lg, n_cast):
    w2_scr, r_scr, m_scr = rest[-3:]
    has_cast = n_cast > 0
    outs = rest[len(rest) - 3 - has_cast - 2:len(rest) - 3]
    merged_ref, s_ref = outs[:2]
    if has_cast:
        _cast_weight_slabs(rest[:n_cast], outs[2])

    @pl.when((pl.program_id(0) == 0) & (pl.program_id(1) == 0))
    def _():
        _build_mixing_matrix(ws_ref, w2_scr, ts, lg)

    @pl.when(pl.program_id(1) == 0)
    def _():
        s_ref[...] = jnp.zeros_like(s_ref)

    for h in range(H_R):
        _retention_head(h, q_ref, k_ref, v_ref, gr_ref, mask_ref, qdec_ref,
                        kdec_ref, sdec_ref, s_ref, r_scr)
    rb = jnp.dot(r_scr[...], wr_ref[...], preferred_element_type=F32)
    vn = _gmlp_norm(vm_ref, lng_ref, lnb_ref)
    for g in range(G_MLP):
        _gmlp_group(g, vn, u_ref, gm_ref, bias_ref, w2_scr, m_scr)
    mb = jnp.dot(m_scr[...], wm_ref[...], preferred_element_type=F32)
    merged = ar_ref[...].astype(F32) * rb + am_ref[...].astype(F32) * mb
    merged_ref[...] = merged.astype(BF16)


def _mixer(qk, v, gates, uv, a_sig, tables, ws, ws_b, ln_g, ln_b, wb,
           layer, batch, seq, state_dest, cast_weights):
    mask, qdec, kdec, sdec = tables
    ts = TS_MIX
    lg = GMLP_CHUNK
    ns = seq // ts
    row = lambda b, s: b * ns + s
    const3 = lambda b, s: (0, 0, 0)
    bias = _mixing_bias(ws_b, layer, ts, lg)
    tile = lambda width, col: pl.BlockSpec((ts, width), lambda b, s: (row(b, s), col))
    w_in_specs, w_out_specs, w_out_shape = _cast_weight_specs(
        cast_weights, layer + 1, batch * ns, row)
    args = [qk, qk, v, gates, gates, uv, uv, a_sig, a_sig, mask, qdec, kdec, sdec,
            ws, bias, ln_g, ln_b, wb, wb] + list(cast_weights)
    aliases = {}
    if state_dest is not None:
        args.append(state_dest)
        aliases = {len(args) - 1: 1}
    outs = pl.pallas_call(
        functools.partial(_mixer_kernel, ts=ts, lg=lg, n_cast=len(cast_weights)),
        grid=(batch, ns),
        in_specs=[
            tile(QKW, 0), tile(QKW, 1),
            tile(RV, 0),
            tile(RV, 0), tile(D_MLP, 1),
            tile(D_MLP, 0), tile(D_MLP, 1),
            tile(D_MODEL, 0), tile(D_MODEL, 1),
            _resident((H_R, ts, ts), const3),
            _resident((H_R, ts, DK), const3),
            _resident((H_R, ts, DK), const3),
            _resident((H_R, 1, DV), const3),
            _resident((None, G_MLP, GMLP_CHUNK, GMLP_CHUNK),
                      lambda b, s: (layer, 0, 0, 0)),
            _resident((ts, G_MLP), lambda b, s: (0, 0)),
            _resident((1, 1, D_MLP), lambda b, s: (layer, 0, 0)),
            _resident((1, 1, D_MLP), lambda b, s: (layer, 0, 0)),
            _resident((None, RV, D_MODEL), lambda b, s: (W_RET, 0, 0)),
            _resident((None, D_MLP, D_MODEL), lambda b, s: (W_MLP, 0, 0)),
        ] + w_in_specs + ([_ANY] if state_dest is not None else []),
        out_specs=[
            pl.BlockSpec((ts, D_MODEL), lambda b, s: (row(b, s), 0)),
            pl.BlockSpec((None, 1, H_R, DK, DV), lambda b, s: (layer, b, 0, 0, 0)),
        ] + w_out_specs,
        out_shape=[
            jax.ShapeDtypeStruct((batch * seq, D_MODEL), BF16),
            jax.ShapeDtypeStruct((DEPTH, batch, H_R, DK, DV), F32),
        ] + w_out_shape,
        input_output_aliases=aliases,
        scratch_shapes=[pltpu.VMEM((G_MLP, ts, ts), BF16),
                        pltpu.VMEM((ts, RV), BF16),
                        pltpu.VMEM((ts, D_MLP), BF16)],
        compiler_params=_params("arbitrary", "arbitrary"),
        name="mixer",
    )(*args)
    return outs[0], outs[1], (outs[2] if cast_weights else None)


def _out_proj_kernel(mg_ref, x_ref, wo_ref, g_ref, *outs, final_norm):
    for c in range(TM_OUT // RC):
        rows = slice(c * RC, (c + 1) * RC)
        y = x_ref[rows, :] + jnp.dot(mg_ref[rows, :], wo_ref[...],
                                     preferred_element_type=F32)
        if final_norm:
            outs[0][rows, :] = _rmsnorm(y, g_ref[0])
        else:
            outs[0][rows, :] = y
            outs[1][rows, :] = _rmsnorm(y, g_ref[0]).astype(BF16)


def _out_proj(merged, x, wb, g, g_layer, total_rows, final_norm):
    rows = merged.shape[0]
    tm = TM_OUT
    out_block = pl.BlockSpec((tm, D_MODEL), lambda i: (i, 0))
    if final_norm:
        out_specs = out_block
        out_shape = jax.ShapeDtypeStruct((rows, D_MODEL), F32)
    else:
        out_specs = [out_block, out_block]
        out_shape = [jax.ShapeDtypeStruct((total_rows, D_MODEL), F32),
                     jax.ShapeDtypeStruct((total_rows, D_MODEL), BF16)]
    return pl.pallas_call(
        functools.partial(_out_proj_kernel, final_norm=final_norm),
        grid=(rows // tm,),
        in_specs=[
            pl.BlockSpec((tm, D_MODEL), lambda i: (i, 0)),
            pl.BlockSpec((tm, D_MODEL), lambda i: (i, 0)),
            _resident((None, D_MODEL, D_MODEL), lambda i: (W_O, 0, 0)),
            pl.BlockSpec((1, 1, D_MODEL), lambda i: (g_layer, 0, 0)),
        ],
        out_specs=out_specs,
        out_shape=out_shape,
        compiler_params=_params("parallel"),
        name="out_proj",
    )(merged, x, wb, g)


def _retention_kernel(q_ref, k_ref, v_ref, gr_ref, mask_ref, qdec_ref, kdec_ref,
                      sdec_ref, s0_ref, *rest, nb, ts):
    o_ref, s_ref = rest[-2:]

    @pl.when(pl.program_id(1) == 0)
    def _():
        s_ref[...] = s0_ref[...]

    for bi in range(nb):
        rows = slice(bi * ts, (bi + 1) * ts)
        for h in range(H_R):
            _retention_head(h, q_ref.at[rows], k_ref.at[rows], v_ref.at[rows],
                            gr_ref.at[rows], mask_ref, qdec_ref, kdec_ref, sdec_ref,
                            s_ref.at[bi:bi + 1], o_ref.at[rows])


def _retention(qk, v, gates, s0_all, tables, layer, row0, batch, seq, state_dest, nb):
    mask, qdec, kdec, sdec = tables
    ns = 1
    ts = seq
    rows = nb * ts
    batch = batch // nb
    r0 = row0 // rows
    row = lambda b, s: r0 + b
    const3 = lambda b, s: (0, 0, 0)
    state_block = pl.BlockSpec((None, nb, H_R, DK, DV), lambda b, s: (layer, b, 0, 0, 0))
    args = [qk, qk, v, gates, mask, qdec, kdec, sdec, s0_all]
    aliases = {}
    if state_dest is not None:
        args.append(state_dest)
        aliases = {len(args) - 1: 1}
    return pl.pallas_call(
        functools.partial(_retention_kernel, nb=nb, ts=ts),
        grid=(batch, ns),
        in_specs=[
            pl.BlockSpec((rows, QKW), lambda b, s: (row(b, s), 0)),
            pl.BlockSpec((rows, QKW), lambda b, s: (row(b, s), 1)),
            pl.BlockSpec((rows, RV), lambda b, s: (row(b, s), 0)),
            pl.BlockSpec((rows, RV), lambda b, s: (row(b, s), 0)),
            pl.BlockSpec((H_R, ts, ts), const3),
            pl.BlockSpec((H_R, ts, DK), const3),
            pl.BlockSpec((H_R, ts, DK), const3),
            pl.BlockSpec((H_R, 1, DV), const3),
            state_block,
        ] + ([_ANY] if state_dest is not None else []),
        out_specs=[
            pl.BlockSpec((rows, RV), lambda b, s: (b, 0)),
            state_block,
        ],
        out_shape=[
            jax.ShapeDtypeStruct((batch * rows, RV), BF16),
            jax.ShapeDtypeStruct(s0_all.shape, F32),
        ],
        input_output_aliases=aliases,
        compiler_params=_params("parallel", "arbitrary"),
        name="retention",
    )(*args)


def _gmlp_kernel(u_ref, vm_ref, gm_ref, ws_ref, bias_ref, lng_ref, lnb_ref,
                 *rest, ts, lg):
    m_ref, vn_ref, w2_scr = rest[-3:]

    @pl.when((pl.program_id(0) == 0) & (pl.program_id(1) == 0))
    def _():
        _build_mixing_matrix(ws_ref, w2_scr, ts, lg)

    vn = _gmlp_norm(vm_ref, lng_ref, lnb_ref)
    vn_ref[...] = vn
    for g in range(G_MLP):
        _gmlp_group(g, vn, u_ref, gm_ref, bias_ref, w2_scr, m_ref)


def _gmlp(uv, gates, ws, ws_b, ln_g, ln_b, layer, row0, batch, seq, ts, vn_dest):
    lg = min(GMLP_CHUNK, seq)
    ns = seq // ts
    t = batch * seq
    r0 = row0 // ts
    row = lambda b, s: r0 + b * ns + s
    args = [uv, uv, gates, ws, _mixing_bias(ws_b, layer, ts, lg), ln_g, ln_b]
    aliases = {}
    if vn_dest is not None:
        args.append(vn_dest)
        aliases = {len(args) - 1: 1}
    return pl.pallas_call(
        functools.partial(_gmlp_kernel, ts=ts, lg=lg),
        grid=(batch, ns),
        in_specs=[
            pl.BlockSpec((ts, D_MLP), lambda b, s: (row(b, s), 0)),
            pl.BlockSpec((ts, D_MLP), lambda b, s: (row(b, s), 1)),
            pl.BlockSpec((ts, D_MLP), lambda b, s: (row(b, s), 1)),
            pl.BlockSpec((None, G_MLP, GMLP_CHUNK, GMLP_CHUNK),
                         lambda b, s: (layer, 0, 0, 0)),
            pl.BlockSpec((ts, G_MLP), lambda b, s: (0, 0)),
            pl.BlockSpec((1, 1, D_MLP), lambda b, s: (layer, 0, 0)),
            pl.BlockSpec((1, 1, D_MLP), lambda b, s: (layer, 0, 0)),
        ] + ([_ANY] if vn_dest is not None else []),
        out_specs=[
            pl.BlockSpec((ts, D_MLP), lambda b, s: (b * ns + s, 0)),
            pl.BlockSpec((None, ts, D_MLP), lambda b, s: (layer, b * ns + s, 0)),
        ],
        out_shape=[jax.ShapeDtypeStruct((t, D_MLP), BF16),
                   jax.ShapeDtypeStruct((DEPTH, t, D_MLP), F32)],
        input_output_aliases=aliases,
        scratch_shapes=[pltpu.VMEM((G_MLP, ts, ts), BF16)],
        compiler_params=_params("arbitrary", "arbitrary"),
        name="gmlp",
    )(*args)


def _merge_out_kernel(r_ref, m_ref, ar_ref, am_ref, x_ref, w_ref, g_ref, *rest,
                      final_norm):
    rb_scr, mg_scr = rest[-2:]
    outs = rest[-3:-2] if final_norm else rest[-4:-2]
    step = pl.program_id(0)

    @pl.when(step == W_RET)
    def _():
        rb_scr[...] = jnp.dot(r_ref[...], w_ref[...], preferred_element_type=F32)

    @pl.when(step == W_MLP)
    def _():
        mb = jnp.dot(m_ref[...], w_ref[...], preferred_element_type=F32)
        merged = ar_ref[...].astype(F32) * rb_scr[...] + am_ref[...].astype(F32) * mb
        mg_scr[...] = merged.astype(BF16)

    @pl.when(step == W_O)
    def _():
        y = x_ref[...] + jnp.dot(mg_scr[...], w_ref[...], preferred_element_type=F32)
        if final_norm:
            outs[0][...] = _rmsnorm(y, g_ref[0])
        else:
            outs[0][...] = y
            outs[1][...] = _rmsnorm(y, g_ref[0]).astype(BF16)


def _merge_out(r, m, a_sig, x, x_row0, wb, g, g_layer, row0, dest, final_norm):
    tm = r.shape[0]
    r0 = row0 // tm
    xr0 = x_row0 // tm
    in_specs = [
        pl.BlockSpec((tm, RV), lambda i: (0, 0)),
        pl.BlockSpec((tm, D_MLP), lambda i: (0, 0)),
        pl.BlockSpec((tm, D_MODEL), lambda i: (r0, 0)),
        pl.BlockSpec((tm, D_MODEL), lambda i: (r0, 1)),
        pl.BlockSpec((tm, D_MODEL), lambda i: (xr0, 0)),
        pl.BlockSpec((None, D_MODEL, D_MODEL), lambda i: (i, 0, 0)),
        pl.BlockSpec((1, 1, D_MODEL), lambda i: (g_layer, 0, 0)),
    ]
    args = [r, m, a_sig, a_sig, x, wb, g]
    if final_norm:
        out_specs = pl.BlockSpec((tm, D_MODEL), lambda i: (0, 0))
        out_shape = jax.ShapeDtypeStruct((tm, D_MODEL), F32)
        aliases = {}
    else:
        out_block = pl.BlockSpec((tm, D_MODEL), lambda i: (r0, 0))
        out_specs = [out_block, out_block]
        out_shape = [jax.ShapeDtypeStruct(dest[0].shape, F32),
                     jax.ShapeDtypeStruct(dest[1].shape, BF16)]
        in_specs += [_ANY, _ANY]
        args += list(dest)
        aliases = {len(args) - 2: 0, len(args) - 1: 1}
    return pl.pallas_call(
        functools.partial(_merge_out_kernel, final_norm=final_norm),
        grid=(3,),
        in_specs=in_specs,
        out_specs=out_specs,
        out_shape=out_shape,
        input_output_aliases=aliases,
        scratch_shapes=[pltpu.VMEM((tm, D_MODEL), F32),
                        pltpu.VMEM((tm, D_MODEL), BF16)],
        compiler_params=_params("arbitrary"),
        name="merge_out",
    )(*args)


def kernel(x_prompt, x_sample, state_ret, norm_g, w_in, ws, ws_b, ln_g, ln_b,
           w_ret_out, w_mlp_out, w_o, final_g):
    b_p, s_p, _ = x_prompt.shape
    b_s, s_s, _ = x_sample.shape
    t_p = b_p * s_p
    t_s = b_s * s_s
    t_all = t_p + t_s
    tab_p = _retention_tables(TS_MIX, CHUNK)
    tab_s = _retention_tables(s_s, s_s)
    cos_p, sin_p = _rotary_tables(s_p, 0)
    cos_s, sin_s = _rotary_tables(s_s, PAST_LEN)
    cos_rows = jnp.concatenate([jnp.tile(cos_p, (b_p, 1)), jnp.tile(cos_s, (b_s, 1))])
    sin_rows = jnp.concatenate([jnp.tile(sin_p, (b_p, 1)), jnp.tile(sin_s, (b_s, 1))])
    norm_g3 = norm_g.reshape(DEPTH, 1, D_MODEL)
    ln_g3 = ln_g.reshape(DEPTH, 1, D_MLP)
    ln_b3 = ln_b.reshape(DEPTH, 1, D_MLP)
    fg3 = final_g.reshape(1, 1, D_MODEL)
    out_weights = (w_ret_out, w_mlp_out, w_o)

    x_p, x_s, xs_row0 = x_prompt.reshape(t_p, D_MODEL), x_sample.reshape(t_s, D_MODEL), 0
    h, wb = _prenorm(x_p, norm_g3, 0, TM_OUT, t_all, 0, cast_weights=out_weights)
    h, _ = _prenorm(x_s, norm_g3, 0, t_s, t_all, t_p, dest=h)
    st_p = st_s = v_s = None
    for l in range(DEPTH):
        last = l == DEPTH - 1
        qk = _in_proj_qk(h, w_in, l, cos_rows, sin_rows)
        v = _in_proj_act(h, w_in, l, (OFF_V,), RV, "copy")
        gates = _in_proj_act(h, w_in, l, (OFF_GR, OFF_GM), RV, "silu")
        uv = _in_proj_act(h, w_in, l, (OFF_U, OFF_VM), D_MLP, "gelu")
        a_sig = _in_proj_act(h, w_in, l, (OFF_AR, OFF_AM), D_MODEL, "sigmoid")

        merged_p, st_p, wb_next = _mixer(qk, v, gates, uv, a_sig, tab_p, ws, ws_b,
                                         ln_g3, ln_b3, wb, l, b_p, s_p, st_p,
                                         () if last else out_weights)
        r_s, st_s = _retention(qk, v, gates, state_ret, tab_s, l, t_p, b_s, s_s, st_s,
                               SAMPLE_SEQS_PER_STEP)
        m_s, v_s = _gmlp(uv, gates, ws, ws_b, ln_g3, ln_b3, l, t_p, b_s, s_s, s_s, v_s)

        g, g_layer = (fg3, 0) if last else (norm_g3, l + 1)
        out_p = _out_proj(merged_p, x_p, wb, g, g_layer, t_all, last)
        out_s = _merge_out(r_s, m_s, a_sig, x_s, xs_row0, wb, g, g_layer, t_p,
                           None if last else out_p, last)
        wb = wb_next
        if last:
            y_p, y_s = out_p, out_s
        else:
            x_all, h = out_s
            x_p, x_s, xs_row0 = x_all, x_all, t_p
    return (y_p.reshape(b_p, s_p, D_MODEL), y_s.reshape(b_s, s_s, D_MODEL),
            st_p, st_s, v_s.reshape(DEPTH, b_s, s_s, D_MLP))
```
